```python
import math
import jax, jax.numpy as jnp
from jax import lax
import numpy as np

D_MODEL = 2048
BATCH = 1
SEQ = 8192
DEPTH = 2
DEC_BATCH = 128
DEC_SEQ = 1
PAST_LEN = 2048
PAGE_SIZE = 128

N_MIXERS = 2
N_DIFF = (DEPTH + 1) // 2
N_MOBA = DEPTH // 2
N_HEADS = 16
DIFF_DK = D_MODEL // N_HEADS // 2
DIFF_DV = 2 * DIFF_DK
MOBA_HD = D_MODEL // N_HEADS
MOBA_BLOCK = 256
MOBA_TOPK = 3
D_FF = 5632
CONV_W = 3
N_BUCKETS = 32
MAX_EXACT = N_BUCKETS // 2
MAX_DISTANCE = 128
DIFF_QBLOCK = 128
MOBA_QBLOCK = 32
EPS = 1e-6
SUBLN_EPS = 1e-5

kernel_name = 'hybrid_diffattn_moba_convffn_step'


def rmsnorm(x, g, eps=EPS):
    x32 = x.astype(jnp.float32)
    y = x32 * lax.rsqrt(jnp.mean(x32 * x32, axis=-1, keepdims=True) + eps)
    return (y * g.astype(jnp.float32)).astype(x.dtype)


def rel_bucket(dist):
    n = jnp.maximum(dist, 0)
    nf = jnp.maximum(n, MAX_EXACT).astype(jnp.float32)
    large = MAX_EXACT + (jnp.log(nf / MAX_EXACT) / math.log(MAX_DISTANCE / MAX_EXACT)
                         * (N_BUCKETS - MAX_EXACT)).astype(jnp.int32)
    large = jnp.minimum(large, N_BUCKETS - 1)
    return jnp.where(n < MAX_EXACT, n, large)


def lambda_init(i):
    return 0.8 - 0.6 * math.exp(-0.3 * i)


def diff_lambda(lam_params, lam_init):
    l = lam_params.astype(jnp.float32)
    return jnp.exp(jnp.sum(l[0] * l[1])) - jnp.exp(jnp.sum(l[2] * l[3])) + lam_init


def diff_core(q, qpos, k, v, kpos, lam, table):
    s = jnp.einsum('qhcd,khcd->chqk', q, k).astype(jnp.float32) * (DIFF_DK ** -0.5)
    dist = qpos[:, None] - kpos[None, :]
    bias = jnp.transpose(table.astype(jnp.float32)[rel_bucket(dist)], (2, 0, 1))
    s = jnp.where(dist >= 0, s + bias, -jnp.inf)
    p = jax.nn.softmax(s, axis=-1)
    a = p[0] - lam * p[1]
    return jnp.einsum('hqk,khd->qhd', a.astype(v.dtype), v)


def diff_project(hn, w_qkv):
    lead = hn.shape[:2]
    qkv = hn @ w_qkv
    q = qkv[..., :D_MODEL].reshape(*lead, N_HEADS, 2, DIFF_DK)
    k = qkv[..., D_MODEL:2 * D_MODEL].reshape(*lead, N_HEADS, 2, DIFF_DK)
    v = qkv[..., 2 * D_MODEL:].reshape(*lead, N_HEADS, DIFF_DV)
    return q, k, v


def diff_output(o, subln_w, lam_init, w_o, dtype):
    o32 = o.astype(jnp.float32)
    o32 = o32 * lax.rsqrt(jnp.mean(o32 * o32, axis=-1, keepdims=True) + SUBLN_EPS)
    o32 = o32 * subln_w.astype(jnp.float32) * (1.0 - lam_init)
    return o32.reshape(*o.shape[:2], N_HEADS * DIFF_DV).astype(dtype) @ w_o


def diff_prompt(hn, w_qkv, lam_params, subln_w, w_o, table, lam_init):
    B, S, _ = hn.shape
    q, k, v = diff_project(hn, w_qkv)
    lam = diff_lambda(lam_params, lam_init)
    pos = jnp.arange(S, dtype=jnp.int32)
    nc = S // DIFF_QBLOCK
    qc = jnp.swapaxes(q.reshape(B, nc, DIFF_QBLOCK, N_HEADS, 2, DIFF_DK), 0, 1)
    pc = pos.reshape(nc, DIFF_QBLOCK)
    attend = jax.vmap(diff_core, in_axes=(0, None, 0, 0, None, None, None))
    o = lax.map(lambda a: attend(a[0], a[1], k, v, pos, lam, table), (qc, pc))
    o = jnp.swapaxes(o, 0, 1).reshape(B, S, N_HEADS, DIFF_DV)
    return diff_output(o, subln_w, lam_init, w_o, hn.dtype), k, v


def diff_sample(hn, cache_k, cache_v, slot, page_table, w_qkv, lam_params, subln_w, w_o, table, lam_init):
    T = hn.shape[1]
    q, k, v = diff_project(hn, w_qkv)
    lam = diff_lambda(lam_params, lam_init)
    qpos = PAST_LEN + jnp.arange(T, dtype=jnp.int32)
    kpos = jnp.arange(PAST_LEN + T, dtype=jnp.int32)

    def one_seq(a):
        qb, kb, vb, pt = a
        kp = cache_k[slot, pt].reshape(-1, N_HEADS, 2, DIFF_DK).astype(kb.dtype)
        vp = cache_v[slot, pt].reshape(-1, N_HEADS, DIFF_DV).astype(vb.dtype)
        return diff_core(qb, qpos, jnp.concatenate([kp, kb], 0), jnp.concatenate([vp, vb], 0), kpos, lam, table)

    o = lax.map(one_seq, (q, k, v, page_table))
    return diff_output(o, subln_w, lam_init, w_o, hn.dtype), k, v


def moba_project(hn, w_qkv):
    lead = hn.shape[:2]
    qkv = hn @ w_qkv
    q = qkv[..., :D_MODEL].reshape(*lead, N_HEADS, MOBA_HD)
    k = qkv[..., D_MODEL:2 * D_MODEL].reshape(*lead, N_HEADS, MOBA_HD)
    v = qkv[..., 2 * D_MODEL:].reshape(*lead, N_HEADS, MOBA_HD)
    return q, k, v


def to_blocks(t):
    L = t.shape[0]
    nb = -(-L // MOBA_BLOCK)
    t = jnp.pad(t, ((0, nb * MOBA_BLOCK - L), (0, 0), (0, 0)))
    return jnp.transpose(t.reshape(nb, MOBA_BLOCK, N_HEADS, MOBA_HD), (2, 0, 1, 3))


def moba_core(q, qpos, kh, vh, km, table):
    nq = q.shape[0]
    nb = kh.shape[1]
    tk = min(MOBA_TOPK, nb)
    own = qpos // MOBA_BLOCK
    gate = jnp.einsum('qhd,hnd->qhn', q.astype(jnp.float32), km)
    past = jnp.arange(nb)[None, :] < own[:, None]
    gate = jnp.where(past[:, None, :], gate, -jnp.inf)
    _, sel = lax.top_k(gate, tk)
    sel_ok = jnp.arange(tk)[None, None, :] < jnp.minimum(own, tk)[:, None, None]
    blk = jnp.concatenate([sel.astype(jnp.int32),
                           jnp.broadcast_to(own[:, None, None], (nq, N_HEADS, 1)).astype(jnp.int32)], axis=-1)
    ok = jnp.concatenate([jnp.broadcast_to(sel_ok, (nq, N_HEADS, tk)),
                          jnp.ones((nq, N_HEADS, 1), dtype=bool)], axis=-1)
    hidx = jnp.arange(N_HEADS)[None, :, None]
    kg = kh[hidx, blk]
    vg = vh[hidx, blk]
    kpos = blk[..., None] * MOBA_BLOCK + jnp.arange(MOBA_BLOCK, dtype=jnp.int32)
    dist = qpos[:, None, None, None] - kpos
    bias = table.astype(jnp.float32)[rel_bucket(dist), hidx[..., None]]
    s = jnp.einsum('qhd,qhbrd->qhbr', q, kg).astype(jnp.float32) * (MOBA_HD ** -0.5) + bias
    s = jnp.where(ok[..., None] & (dist >= 0), s, -jnp.inf)
    p = jax.nn.softmax(s.reshape(nq, N_HEADS, -1), axis=-1).reshape(s.shape)
    return jnp.einsum('qhbr,qhbrd->qhd', p.astype(vg.dtype), vg)


def moba_prompt(hn, w_qkv, w_o, table):
    B, S, _ = hn.shape
    q, k, v = moba_project(hn, w_qkv)
    kh = jax.vmap(to_blocks)(k)
    vh = jax.vmap(to_blocks)(v)
    km = jnp.mean(kh.astype(jnp.float32), axis=3)
    pos = jnp.arange(S, dtype=jnp.int32)
    nc = S // MOBA_QBLOCK
    qc = jnp.swapaxes(q.reshape(B, nc, MOBA_QBLOCK, N_HEADS, MOBA_HD), 0, 1)
    pc = pos.reshape(nc, MOBA_QBLOCK)
    attend = jax.vmap(moba_core, in_axes=(0, None, 0, 0, 0, None))
    o = lax.map(lambda a: attend(a[0], a[1], kh, vh, km, table), (qc, pc))
    o = jnp.swapaxes(o, 0, 1).reshape(B, S, D_MODEL)
    return o.astype(hn.dtype) @ w_o, k, v


def moba_sample(hn, cache_k, cache_v, slot, page_table, w_qkv, w_o, table):
    Bd, T, _ = hn.shape
    q, k, v = moba_project(hn, w_qkv)
    qpos = PAST_LEN + jnp.arange(T, dtype=jnp.int32)

    def one_seq(a):
        qb, kb, vb, pt = a
        kk = jnp.concatenate([cache_k[slot, pt].reshape(-1, N_HEADS, MOBA_HD).astype(kb.dtype), kb], 0)
        vv = jnp.concatenate([cache_v[slot, pt].reshape(-1, N_HEADS, MOBA_HD).astype(vb.dtype), vb], 0)
        kh = to_blocks(kk)
        vh = to_blocks(vv)
        km = jnp.mean(kh.astype(jnp.float32), axis=2)
        return moba_core(qb, qpos, kh, vh, km, table)

    o = lax.map(one_seq, (q, k, v, page_table)).reshape(Bd, T, D_MODEL)
    return o.astype(hn.dtype) @ w_o, k, v


def conv_ffn(hn, w_up, conv_w, conv_b, w_down, prev):
    hid = hn @ w_up
    g, u = hid[..., :D_FF], hid[..., D_FF:]
    T = g.shape[1]
    gp = jnp.concatenate([prev.astype(g.dtype), g], axis=1)
    gc = conv_b + conv_w[CONV_W - 1] * g
    for j in range(CONV_W - 1):
        gc = gc + conv_w[j] * gp[:, j:j + T]
    return (jax.nn.silu(gc) * u) @ w_down, gp[:, T:]


def setup_inputs(seed: int = 0) -> dict:
    key = jax.random.key(seed)
    ks = jax.random.split(key, 24)
    f32 = jnp.float32
    n_pages = PAST_LEN // PAGE_SIZE
    n_used = DEC_BATCH * n_pages
    n_phys = n_used + n_used // 4
    page_table = jax.random.permutation(ks[7], n_phys)[:n_used].reshape(DEC_BATCH, n_pages).astype(jnp.int32)
    return {
        'x_prompt': jax.random.normal(ks[0], (BATCH, SEQ, D_MODEL), f32),
        'x_sample': jax.random.normal(ks[1], (DEC_BATCH, DEC_SEQ, D_MODEL), f32),
        'cache_k_diff': jax.random.normal(ks[2], (N_DIFF, n_phys, PAGE_SIZE, N_HEADS, 2, DIFF_DK), f32),
        'cache_v_diff': jax.random.normal(ks[3], (N_DIFF, n_phys, PAGE_SIZE, N_HEADS, DIFF_DV), f32),
        'cache_k_moba': jax.random.normal(ks[4], (N_MOBA, n_phys, PAGE_SIZE, N_HEADS, MOBA_HD), f32),
        'cache_v_moba': jax.random.normal(ks[5], (N_MOBA, n_phys, PAGE_SIZE, N_HEADS, MOBA_HD), f32),
        'state_conv': jax.random.normal(ks[6], (DEPTH, DEC_BATCH, CONV_W - 1, D_FF), f32),
        'page_table': page_table,
        'rel_bias': jax.random.normal(ks[8], (N_BUCKETS, N_HEADS), f32) * 0.5,
        'norm_mix': 1.0 + 0.02 * jax.random.normal(ks[9], (DEPTH, D_MODEL), f32),
        'norm_ffn': 1.0 + 0.02 * jax.random.normal(ks[10], (DEPTH, D_MODEL), f32),
        'norm_final': 1.0 + 0.02 * jax.random.normal(ks[11], (D_MODEL,), f32),
        'w_qkv_diff': jax.random.normal(ks[12], (N_DIFF, D_MODEL, 3 * D_MODEL), f32) * D_MODEL ** -0.5,
        'lambda_diff': jax.random.normal(ks[13], (N_DIFF, 4, DIFF_DK), f32) * 0.1,
        'subln_diff': 1.0 + 0.02 * jax.random.normal(ks[14], (N_DIFF, DIFF_DV), f32),
        'w_o_diff': jax.random.normal(ks[15], (N_DIFF, D_MODEL, D_MODEL), f32) * D_MODEL ** -0.5,
        'w_qkv_moba': jax.random.normal(ks[16], (N_MOBA, D_MODEL, 3 * D_MODEL), f32) * D_MODEL ** -0.5,
        'w_o_moba': jax.random.normal(ks[17], (N_MOBA, D_MODEL, D_MODEL), f32) * D_MODEL ** -0.5,
        'w_up': jax.random.normal(ks[18], (DEPTH, D_MODEL, 2 * D_FF), f32) * D_MODEL ** -0.5,
        'conv_w': jax.random.normal(ks[19], (DEPTH, CONV_W, D_FF), f32) * 0.5,
        'conv_b': jax.random.normal(ks[20], (DEPTH, D_FF), f32) * 0.02,
        'w_down': jax.random.normal(ks[21], (DEPTH, D_FF, D_MODEL), f32) * D_FF ** -0.5,
    }


def reference(x_prompt, x_sample, cache_k_diff, cache_v_diff, cache_k_moba, cache_v_moba, state_conv,
              page_table, rel_bias, norm_mix, norm_ffn, norm_final, w_qkv_diff, lambda_diff, subln_diff,
              w_o_diff, w_qkv_moba, w_o_moba, w_up, conv_w, conv_b, w_down):
    hp, hs = x_prompt, x_sample
    kdp, vdp, kmp, vmp, cvp = [], [], [], [], []
    kds, vds, kms, vms, cvs = [], [], [], [], []
    for i in range(DEPTH):
        slot = i // N_MIXERS
        np_ = rmsnorm(hp, norm_mix[i])
        ns = rmsnorm(hs, norm_mix[i])
        if i % N_MIXERS == 0:
            li = lambda_init(i)
            mp, k_p, v_p = diff_prompt(np_, w_qkv_diff[slot], lambda_diff[slot], subln_diff[slot],
                                       w_o_diff[slot], rel_bias, li)
            ms, k_s, v_s = diff_sample(ns, cache_k_diff, cache_v_diff, slot, page_table, w_qkv_diff[slot],
                                       lambda_diff[slot], subln_diff[slot], w_o_diff[slot], rel_bias, li)
            kdp.append(k_p); vdp.append(v_p); kds.append(k_s); vds.append(v_s)
        else:
            mp, k_p, v_p = moba_prompt(np_, w_qkv_moba[slot], w_o_moba[slot], rel_bias)
            ms, k_s, v_s = moba_sample(ns, cache_k_moba, cache_v_moba, slot, page_table,
                                       w_qkv_moba[slot], w_o_moba[slot], rel_bias)
            kmp.append(k_p); vmp.append(v_p); kms.append(k_s); vms.append(v_s)
        hp = hp + mp
        hs = hs + ms
        prev_p = jnp.zeros((hp.shape[0], CONV_W - 1, D_FF), hp.dtype)
        fp, cp = conv_ffn(rmsnorm(hp, norm_ffn[i]), w_up[i], conv_w[i], conv_b[i], w_down[i], prev_p)
        fs, cs = conv_ffn(rmsnorm(hs, norm_ffn[i]), w_up[i], conv_w[i], conv_b[i], w_down[i], state_conv[i])
        hp = hp + fp
        hs = hs + fs
        cvp.append(cp)
        cvs.append(cs)
    y_prompt = rmsnorm(hp, norm_final)
    y_sample = rmsnorm(hs, norm_final)
    return (y_prompt, y_sample,
            jnp.stack(kdp), jnp.stack(vdp), jnp.stack(kmp), jnp.stack(vmp), jnp.stack(cvp),
            jnp.stack(kds), jnp.stack(vds), jnp.stack(kms), jnp.stack(vms), jnp.stack(cvs))
```

```python
import functools
import math

import jax
import jax.numpy as jnp
from jax import lax
from jax.experimental import pallas as pl
from jax.experimental.pallas import tpu as pltpu

F32 = jnp.float32
BF16 = jnp.bfloat16

N_HEADS = 16
HEAD_DIM = 128
DIFF_DK = HEAD_DIM // 2
MOBA_BLOCK = 256
MOBA_TOPK = 3
PAGE_SIZE = 128
CONV_W = 3
N_BUCKETS = 32
MAX_EXACT = N_BUCKETS // 2
MAX_DISTANCE = 128
EPS = 1e-6
SUBLN_EPS = 1e-5
NEG = -1e30
ATT_TILE = 256
VMEM_LIMIT = 56 * 1024 * 1024

_NT = (((1,), (1,)), ((), ()))


def _lambda_init(i):
    return 0.8 - 0.6 * math.exp(-0.3 * i)


def _params(n_grid_dims):
    return pltpu.CompilerParams(dimension_semantics=("arbitrary",) * n_grid_dims,
                                vmem_limit_bytes=VMEM_LIMIT)


def _norm_rows(x, g):
    r = lax.rsqrt(jnp.mean(x * x, axis=-1, keepdims=True) + EPS)
    return x * r * g


def _norm_matmul_kernel(x_ref, g_ref, w_ref, o_ref, xn_ref):
    @pl.when(pl.program_id(1) == 0)
    def _():
        xn_ref[...] = _norm_rows(x_ref[...], g_ref[...]).astype(BF16)

    o_ref[0] = jnp.dot(xn_ref[...], w_ref[...], preferred_element_type=F32)


def _norm_qkv(x, g, w, *, bm, bn):
    m, d = x.shape
    nb = d // bn
    return pl.pallas_call(
        _norm_matmul_kernel,
        grid=(m // bm, 3 * nb),
        in_specs=[pl.BlockSpec((bm, d), lambda i, j: (i, 0)),
                  pl.BlockSpec((1, d), lambda i, j: (0, 0)),
                  pl.BlockSpec((d, bn), lambda i, j: (0, j))],
        out_specs=pl.BlockSpec((1, bm, bn), lambda i, j: (j // nb, i, j % nb)),
        out_shape=jax.ShapeDtypeStruct((3, m, d), F32),
        scratch_shapes=[pltpu.VMEM((bm, d), BF16)],
        compiler_params=_params(2),
        name="norm_qkv",
    )(x, g.reshape(1, d), w)


def _mm_res_kernel(a_ref, w_ref, r_ref, o_ref):
    o_ref[...] = r_ref[...] + jnp.dot(a_ref[...], w_ref[...], preferred_element_type=F32)


def _matmul_residual(a, w, res, *, bm, bn):
    m, k = a.shape
    n = w.shape[1]
    return pl.pallas_call(
        _mm_res_kernel,
        grid=(m // bm, n // bn),
        in_specs=[pl.BlockSpec((bm, k), lambda i, j: (i, 0)),
                  pl.BlockSpec((k, bn), lambda i, j: (0, j)),
                  pl.BlockSpec((bm, bn), lambda i, j: (i, j))],
        out_specs=pl.BlockSpec((bm, bn), lambda i, j: (i, j)),
        out_shape=jax.ShapeDtypeStruct((m, n), F32),
        compiler_params=_params(2),
        name="matmul_residual",
    )(a, w, res)


def _rmsnorm_kernel(x_ref, g_ref, o_ref):
    o_ref[...] = _norm_rows(x_ref[...], g_ref[...])


def _rmsnorm(x, g, *, bm):
    m, d = x.shape
    return pl.pallas_call(
        _rmsnorm_kernel,
        grid=(m // bm,),
        in_specs=[pl.BlockSpec((bm, d), lambda i: (i, 0)),
                  pl.BlockSpec((1, d), lambda i: (0, 0))],
        out_specs=pl.BlockSpec((bm, d), lambda i: (i, 0)),
        out_shape=jax.ShapeDtypeStruct((m, d), F32),
        compiler_params=_params(1),
        name="final_rmsnorm",
    )(x, g.reshape(1, d))


def _silu_gate(gc, u):
    return gc * (1.0 / (1.0 + jnp.exp(-gc))) * u


def _ffn_up_seq_kernel(x_ref, g_ref, wg_ref, wu_ref, cw_ref, cb_ref, h_ref, st_ref,
                       xn_ref, gbuf_ref, carry_ref, *, bm):
    i = pl.program_id(0)
    j = pl.program_id(1)

    @pl.when(j == 0)
    def _():
        xn_ref[...] = _norm_rows(x_ref[...], g_ref[...]).astype(BF16)

    xn = xn_ref[...]
    g = jnp.dot(xn, wg_ref[...], preferred_element_type=F32)
    u = jnp.dot(xn, wu_ref[...], preferred_element_type=F32)

    @pl.when(i == 0)
    def _():
        gbuf_ref[0:8, :] = jnp.zeros((8, g.shape[1]), F32)

    @pl.when(i > 0)
    def _():
        gbuf_ref[0:8, :] = carry_ref[j]

    gbuf_ref[8:8 + bm, :] = g
    g1 = gbuf_ref[7:7 + bm, :]
    g2 = gbuf_ref[6:6 + bm, :]
    cw = cw_ref[...]
    gc = cb_ref[...] + cw[2:3] * g
    gc = gc + cw[0:1] * g2
    gc = gc + cw[1:2] * g1
    h_ref[...] = _silu_gate(gc, u).astype(BF16)
    last8 = gbuf_ref[bm:bm + 8, :]
    carry_ref[j] = last8
    st_ref[0] = last8


def _ffn_up_seq(x, g, w_up, conv_w, conv_b, *, bm, bn):
    m, d = x.shape
    dff = conv_w.shape[1]
    nj = dff // bn
    h, st = pl.pallas_call(
        functools.partial(_ffn_up_seq_kernel, bm=bm),
        grid=(m // bm, nj),
        in_specs=[pl.BlockSpec((bm, d), lambda i, j: (i, 0)),
                  pl.BlockSpec((1, d), lambda i, j: (0, 0)),
                  pl.BlockSpec((d, bn), lambda i, j: (0, j)),
                  pl.BlockSpec((d, bn), lambda i, j: (0, j + nj)),
                  pl.BlockSpec((CONV_W, bn), lambda i, j: (0, j)),
                  pl.BlockSpec((1, bn), lambda i, j: (0, j))],
        out_specs=[pl.BlockSpec((bm, bn), lambda i, j: (i, j)),
                   pl.BlockSpec((1, 8, bn), lambda i, j: (i, 0, j))],
        out_shape=[jax.ShapeDtypeStruct((m, dff), BF16),
                   jax.ShapeDtypeStruct((m // bm, 8, dff), F32)],
        scratch_shapes=[pltpu.VMEM((bm, d), BF16),
                        pltpu.VMEM((bm + 8, bn), F32),
                        pltpu.VMEM((nj, 8, bn), F32)],
        compiler_params=_params(2),
        name="ffn_up_seq",
    )(x, g.reshape(1, d), w_up, w_up, conv_w, conv_b.reshape(1, dff))
    return h, st[-1, 8 - (CONV_W - 1):]


def _ffn_up_tok_kernel(x_ref, g_ref, wg_ref, wu_ref, cw_ref, cb_ref, p0_ref, p1_ref,
                       h_ref, gout_ref, xn_ref):
    @pl.when(pl.program_id(0) == 0)
    def _():
        xn_ref[...] = _norm_rows(x_ref[...], g_ref[...]).astype(BF16)

    xn = xn_ref[...]
    g = jnp.dot(xn, wg_ref[...], preferred_element_type=F32)
    u = jnp.dot(xn, wu_ref[...], preferred_element_type=F32)
    cw = cw_ref[...]
    gc = cb_ref[...] + cw[2:3] * g
    gc = gc + cw[0:1] * p0_ref[...]
    gc = gc + cw[1:2] * p1_ref[...]
    h_ref[...] = _silu_gate(gc, u).astype(BF16)
    gout_ref[...] = g


def _ffn_up_tok(x, g, w_up, conv_w, conv_b, prev, *, bn):
    m, d = x.shape
    dff = conv_w.shape[1]
    nj = dff // bn
    h, gout = pl.pallas_call(
        _ffn_up_tok_kernel,
        grid=(nj,),
        in_specs=[pl.BlockSpec((m, d), lambda j: (0, 0)),
                  pl.BlockSpec((1, d), lambda j: (0, 0)),
                  pl.BlockSpec((d, bn), lambda j: (0, j)),
                  pl.BlockSpec((d, bn), lambda j: (0, j + nj)),
                  pl.BlockSpec((CONV_W, bn), lambda j: (0, j)),
                  pl.BlockSpec((1, bn), lambda j: (0, j)),
                  pl.BlockSpec((m, bn), lambda j: (0, j)),
                  pl.BlockSpec((m, bn), lambda j: (0, j))],
        out_specs=[pl.BlockSpec((m, bn), lambda j: (0, j)),
                   pl.BlockSpec((m, bn), lambda j: (0, j))],
        out_shape=[jax.ShapeDtypeStruct((m, dff), BF16),
                   jax.ShapeDtypeStruct((m, dff), F32)],
        scratch_shapes=[pltpu.VMEM((m, d), BF16)],
        compiler_params=_params(1),
        name="ffn_up_tok",
    )(x, g.reshape(1, d), w_up, w_up, conv_w, conv_b.reshape(1, dff), prev[:, 0], prev[:, 1])
    return h, jnp.stack([prev[:, 1], gout], axis=1)


def _rel_bucket(dist):
    n = jnp.maximum(dist, 0)
    nf = jnp.maximum(n, MAX_EXACT).astype(F32)
    large = MAX_EXACT + (jnp.log(nf / MAX_EXACT) / math.log(MAX_DISTANCE / MAX_EXACT)
                         * (N_BUCKETS - MAX_EXACT)).astype(jnp.int32)
    large = jnp.minimum(large, N_BUCKETS - 1)
    return jnp.where(n < MAX_EXACT, n, large)


def _prompt_bias_tiles(rel_bias, t):
    assert t >= MAX_DISTANCE
    far = rel_bias[N_BUCKETS - 1]
    by_dist = (rel_bias[_rel_bucket(jnp.arange(2 * t))] - far[None, :]).T
    dd = jnp.arange(t)[None, :] - jnp.arange(t)[:, None]
    diag = jnp.where(dd >= 0, by_dist[:, jnp.maximum(dd, 0)], NEG)
    sub = by_dist[:, t + dd]
    return diag.astype(F32), sub.astype(F32)


def _decode_bias_tables(rel_bias, past_len):
    far = rel_bias[N_BUCKETS - 1]
    dist = past_len - jnp.arange(past_len)
    pages = (rel_bias[_rel_bucket(dist)] - far[None, :]).reshape(past_len // PAGE_SIZE, PAGE_SIZE, N_HEADS)
    self_bias = (rel_bias[0] - far).reshape(N_HEADS, 1)
    return jnp.transpose(pages, (0, 2, 1)).astype(F32), self_bias.astype(F32)


def _diff_lambda(lp, lam_init):
    a = jnp.sum(lp[0:1] * lp[1:2], axis=-1, keepdims=True)
    b = jnp.sum(lp[2:3] * lp[3:4], axis=-1, keepdims=True)
    return jnp.exp(a) - jnp.exp(b) + lam_init


def _flash_first(s, vt, m_ref, l_ref, acc_ref):
    m = jnp.max(s, axis=0, keepdims=True)
    p = jnp.exp(s - m)
    m_ref[...] = m
    l_ref[...] = jnp.sum(p, axis=0, keepdims=True)
    acc_ref[...] = jnp.dot(vt, p.astype(BF16), preferred_element_type=F32)


def _flash_next(s, vt, m_ref, l_ref, acc_ref):
    m_prev = m_ref[...]
    m = jnp.maximum(m_prev, jnp.max(s, axis=0, keepdims=True))
    p = jnp.exp(s - m)
    alpha = jnp.exp(m_prev - m)
    m_ref[...] = m
    l_ref[...] = alpha * l_ref[...] + jnp.sum(p, axis=0, keepdims=True)
    acc_ref[...] = alpha * acc_ref[...] + jnp.dot(vt, p.astype(BF16), preferred_element_type=F32)


def _diff_prompt_kernel(q_ref, k_ref, v_ref, bd_ref, bs_ref, lp_ref, sw_ref, o_ref,
                        kb_ref, vt_ref, m_ref, l_ref, acc_ref, *, t, lam_init):
    i = pl.program_id(1)
    n_tiles = kb_ref.shape[0]

    @pl.when(i == 0)
    def _():
        def fill(c, carry):
            r0 = pl.multiple_of(c * t, t)
            kb_ref[c] = k_ref[0, pl.ds(r0, t), :].astype(BF16)
            vt_ref[c] = v_ref[0, pl.ds(r0, t), :].T.astype(BF16)
            return carry
        lax.fori_loop(0, n_tiles, fill, 0)

    q = q_ref[0] * (DIFF_DK ** -0.5)
    lane = lax.broadcasted_iota(jnp.int32, q.shape, 1)
    q0 = jnp.where(lane < DIFF_DK, q, 0.0)
    q1 = jnp.where(lane >= DIFF_DK, q, 0.0)
    qt = jnp.concatenate([q0.T, q1.T], axis=1).astype(BF16)

    def scores(j):
        return jnp.dot(kb_ref[j], qt, preferred_element_type=F32)

    bd = bd_ref[0]
    _flash_first(scores(i) + jnp.concatenate([bd, bd], axis=1), vt_ref[i], m_ref, l_ref, acc_ref)

    @pl.when(i > 0)
    def _():
        bs = bs_ref[0]
        _flash_next(scores(i - 1) + jnp.concatenate([bs, bs], axis=1), vt_ref[i - 1],
                    m_ref, l_ref, acc_ref)

    def far(j, carry):
        _flash_next(scores(j), vt_ref[j], m_ref, l_ref, acc_ref)
        return carry
    lax.fori_loop(0, i - 1, far, 0)

    o2 = acc_ref[...] * (1.0 / l_ref[...])
    lam = _diff_lambda(lp_ref[...], lam_init)
    o = o2[:, :t] - lam * o2[:, t:]
    o = o * lax.rsqrt(jnp.mean(o * o, axis=0, keepdims=True) + SUBLN_EPS)
    o_ref[...] = (o.T * sw_ref[...] * (1.0 - lam_init)).astype(BF16)


def _diff_prompt_attention(qkv, bias_diag, bias_sub, lam_params, subln_w, lam_init):
    _, s, d = qkv.shape
    t = ATT_TILE
    nt = s // t
    return pl.pallas_call(
        functools.partial(_diff_prompt_kernel, t=t, lam_init=lam_init),
        grid=(N_HEADS, nt),
        in_specs=[pl.BlockSpec((1, t, HEAD_DIM), lambda h, i: (0, i, h)),
                  pl.BlockSpec((1, s, HEAD_DIM), lambda h, i: (1, 0, h)),
                  pl.BlockSpec((1, s, HEAD_DIM), lambda h, i: (2, 0, h)),
                  pl.BlockSpec((1, t, t), lambda h, i: (h, 0, 0)),
                  pl.BlockSpec((1, t, t), lambda h, i: (h, 0, 0)),
                  pl.BlockSpec((4, DIFF_DK), lambda h, i: (0, 0)),
                  pl.BlockSpec((1, HEAD_DIM), lambda h, i: (0, 0))],
        out_specs=pl.BlockSpec((t, HEAD_DIM), lambda h, i: (i, h)),
        out_shape=jax.ShapeDtypeStruct((s, d), BF16),
        scratch_shapes=[pltpu.VMEM((nt, t, HEAD_DIM), BF16),
                        pltpu.VMEM((nt, HEAD_DIM, t), BF16),
                        pltpu.VMEM((1, 2 * t), F32),
                        pltpu.VMEM((1, 2 * t), F32),
                        pltpu.VMEM((HEAD_DIM, 2 * t), F32)],
        compiler_params=_params(2),
        name="diff_prompt_attention",
    )(qkv, qkv, qkv, bias_diag, bias_sub, lam_params, subln_w.reshape(1, HEAD_DIM))


def _block_mean_kernel(k_ref, o_ref):
    o_ref[0] = jnp.sum(k_ref[0], axis=0, keepdims=True) * (1.0 / MOBA_BLOCK)


def _block_means(qkv):
    _, s, d = qkv.shape
    nb = s // MOBA_BLOCK
    out = pl.pallas_call(
        _block_mean_kernel,
        grid=(nb,),
        in_specs=[pl.BlockSpec((1, MOBA_BLOCK, d), lambda n: (1, n, 0))],
        out_specs=pl.BlockSpec((1, 1, d), lambda n: (n, 0, 0)),
        out_shape=jax.ShapeDtypeStruct((nb, 1, d), F32),
        compiler_params=_params(1),
        name="moba_block_means",
    )(qkv)
    return out.reshape(nb, d)


def _split_bf16(x):
    hi = x.astype(BF16)
    lo = (x - hi.astype(F32)).astype(BF16)
    return hi, lo


def _moba_prompt_kernel(q_ref, k_ref, v_ref, km_ref, bd_ref, bs_ref, o_ref,
                        ka_ref, vt_ref, m_ref, l_ref, acc_ref, *, t):
    i = pl.program_id(1)
    n_tiles = ka_ref.shape[0]
    nb = km_ref.shape[0]

    @pl.when(i == 0)
    def _():
        lane = lax.broadcasted_iota(jnp.int32, (t, HEAD_DIM), 1)

        def fill(c, carry):
            r0 = pl.multiple_of(c * t, t)
            ka_ref[c, :, 0:HEAD_DIM] = k_ref[0, pl.ds(r0, t), :].astype(BF16)
            ka_ref[c, :, HEAD_DIM:] = jnp.where(lane == c, 1.0, 0.0).astype(BF16)
            vt_ref[c] = v_ref[0, pl.ds(r0, t), :].T.astype(BF16)
            return carry
        lax.fori_loop(0, n_tiles, fill, 0)

    qt = q_ref[0].T

    km_hi, km_lo = _split_bf16(km_ref[...])
    qt_hi, qt_lo = _split_bf16(qt)
    gate = (jnp.dot(km_hi, qt_hi, preferred_element_type=F32)
            + jnp.dot(km_hi, qt_lo, preferred_element_type=F32)
            + jnp.dot(km_lo, qt_hi, preferred_element_type=F32))
    row = lax.broadcasted_iota(jnp.int32, (nb, t), 0)
    rowf = row.astype(F32)
    g = jnp.where(row < i, gate, NEG)
    sel = row == i
    for _ in range(MOBA_TOPK):
        mx = jnp.max(g, axis=0, keepdims=True)
        is_max = (g == mx) & (mx > 0.5 * NEG)
        first = jnp.min(jnp.where(is_max, rowf, float(nb)), axis=0, keepdims=True)
        pick = rowf == first
        sel = sel | pick
        g = jnp.where(pick, NEG, g)
    mask_rows = jnp.where(sel, 0.0, NEG)
    qa = jnp.concatenate([qt * (HEAD_DIM ** -0.5), mask_rows,
                          jnp.zeros((HEAD_DIM - nb, t), F32)], axis=0).astype(BF16)

    def scores(j):
        return jnp.dot(ka_ref[j], qa, preferred_element_type=F32)

    _flash_first(scores(i) + bd_ref[0], vt_ref[i], m_ref, l_ref, acc_ref)

    @pl.when(i > 0)
    def _():
        _flash_next(scores(i - 1) + bs_ref[0], vt_ref[i - 1], m_ref, l_ref, acc_ref)

    def far(j, carry):
        _flash_next(scores(j), vt_ref[j], m_ref, l_ref, acc_ref)
        return carry
    lax.fori_loop(0, i - 1, far, 0)

    o = acc_ref[...] * (1.0 / l_ref[...])
    o_ref[...] = o.T.astype(BF16)


def _moba_prompt_attention(qkv, km, bias_diag, bias_sub):
    _, s, d = qkv.shape
    t = MOBA_BLOCK
    nt = s // t
    assert nt <= HEAD_DIM
    return pl.pallas_call(
        functools.partial(_moba_prompt_kernel, t=t),
        grid=(N_HEADS, nt),
        in_specs=[pl.BlockSpec((1, t, HEAD_DIM), lambda h, i: (0, i, h)),
                  pl.BlockSpec((1, s, HEAD_DIM), lambda h, i: (1, 0, h)),
                  pl.BlockSpec((1, s, HEAD_DIM), lambda h, i: (2, 0, h)),
                  pl.BlockSpec((nt, HEAD_DIM), lambda h, i: (0, h)),
                  pl.BlockSpec((1, t, t), lambda h, i: (h, 0, 0)),
                  pl.BlockSpec((1, t, t), lambda h, i: (h, 0, 0))],
        out_specs=pl.BlockSpec((t, HEAD_DIM), lambda h, i: (i, h)),
        out_shape=jax.ShapeDtypeStruct((s, d), BF16),
        scratch_shapes=[pltpu.VMEM((nt, t, 2 * HEAD_DIM), BF16),
                        pltpu.VMEM((nt, HEAD_DIM, t), BF16),
                        pltpu.VMEM((1, t), F32),
                        pltpu.VMEM((1, t), F32),
                        pltpu.VMEM((HEAD_DIM, t), F32)],
        compiler_params=_params(2),
        name="moba_prompt_attention",
    )(qkv, qkv, qkv, km, bias_diag, bias_sub)


def _head_of_lane(shape):
    return lax.broadcasted_iota(jnp.int32, shape, 1) // HEAD_DIM


def _diff_decode_kernel(pt_ref, q_ref, kn_ref, vn_ref, kc_ref, vc_ref, bp_ref, b0_ref, lp_ref, sw_ref,
                        o_ref, qr_ref, m_ref, l_ref, acc_ref, *, lam_init):
    p = pl.program_id(1)
    d = q_ref.shape[-1]
    rows = 2 * N_HEADS

    @pl.when(p == 0)
    def _():
        row = lax.broadcasted_iota(jnp.int32, (rows, d), 0)
        lane = lax.broadcasted_iota(jnp.int32, (rows, d), 1)
        seg = lane // DIFF_DK == (row % N_HEADS) * 2 + row // N_HEADS
        qr = jnp.where(seg, q_ref[0, 0] * (DIFF_DK ** -0.5), 0.0)
        qr_ref[...] = qr.astype(BF16)
        m_ref[...] = jnp.sum(qr * kn_ref[0, 0], axis=1, keepdims=True) + b0_ref[...]
        l_ref[...] = jnp.ones((rows, 1), F32)
        acc_ref[...] = jnp.broadcast_to(vn_ref[0, 0], (rows, d))

    s = lax.dot_general(qr_ref[...], kc_ref[0].astype(BF16), _NT,
                        preferred_element_type=F32) + bp_ref[0]
    m_prev = m_ref[...]
    m = jnp.maximum(m_prev, jnp.max(s, axis=1, keepdims=True))
    pr = jnp.exp(s - m)
    alpha = jnp.exp(m_prev - m)
    m_ref[...] = m
    l_ref[...] = alpha * l_ref[...] + jnp.sum(pr, axis=1, keepdims=True)
    acc_ref[...] = alpha * acc_ref[...] + jnp.dot(pr.astype(BF16), vc_ref[0].astype(BF16),
                                                  preferred_element_type=F32)

    @pl.when(p == pl.num_programs(1) - 1)
    def _():
        o2 = acc_ref[...] * (1.0 / l_ref[...])
        lam = _diff_lambda(lp_ref[...], lam_init)
        od = o2[:N_HEADS] - lam * o2[N_HEADS:]
        own = _head_of_lane((N_HEADS, d)) == lax.broadcasted_iota(jnp.int32, (N_HEADS, d), 0)
        od = jnp.where(own, od, 0.0)
        ss = jnp.sum(od * od, axis=1, keepdims=True) * (1.0 / HEAD_DIM)
        od = od * lax.rsqrt(ss + SUBLN_EPS)
        o = jnp.sum(od, axis=0, keepdims=True) * sw_ref[...] * (1.0 - lam_init)
        o_ref[0] = o.astype(BF16)


def _diff_decode_attention(qkv, cache_k, cache_v, page_table, bias_pages, bias_self,
                           lam_params, subln_w, lam_init):
    _, b, d = qkv.shape
    n_pages = page_table.shape[1]
    n_phys = cache_k.shape[0]
    rows = 2 * N_HEADS
    qkv4 = qkv.reshape(3, b, 1, d)
    kc = cache_k.reshape(n_phys, PAGE_SIZE, d)
    vc = cache_v.reshape(n_phys, PAGE_SIZE, d)
    bp = jnp.concatenate([bias_pages, bias_pages], axis=1)
    b0 = jnp.concatenate([bias_self, bias_self], axis=0)
    sw = jnp.tile(subln_w.reshape(1, HEAD_DIM), (1, N_HEADS))

    def page(bi, p, pt):
        return (pt[bi * n_pages + p], 0, 0)

    out = pl.pallas_call(
        functools.partial(_diff_decode_kernel, lam_init=lam_init),
        grid_spec=pltpu.PrefetchScalarGridSpec(
            num_scalar_prefetch=1,
            grid=(b, n_pages),
            in_specs=[pl.BlockSpec((1, 1, 1, d), lambda bi, p, pt: (0, bi, 0, 0)),
                      pl.BlockSpec((1, 1, 1, d), lambda bi, p, pt: (1, bi, 0, 0)),
                      pl.BlockSpec((1, 1, 1, d), lambda bi, p, pt: (2, bi, 0, 0)),
                      pl.BlockSpec((1, PAGE_SIZE, d), page),
                      pl.BlockSpec((1, PAGE_SIZE, d), page),
                      pl.BlockSpec((1, rows, PAGE_SIZE), lambda bi, p, pt: (p, 0, 0)),
                      pl.BlockSpec((rows, 1), lambda bi, p, pt: (0, 0)),
                      pl.BlockSpec((4, DIFF_DK), lambda bi, p, pt: (0, 0)),
                      pl.BlockSpec((1, d), lambda bi, p, pt: (0, 0))],
            out_specs=pl.BlockSpec((1, 1, d), lambda bi, p, pt: (bi, 0, 0)),
            scratch_shapes=[pltpu.VMEM((rows, d), BF16),
                            pltpu.VMEM((rows, 1), F32),
                            pltpu.VMEM((rows, 1), F32),
                            pltpu.VMEM((rows, d), F32)]),
        out_shape=jax.ShapeDtypeStruct((b, 1, d), BF16),
        compiler_params=_params(2),
        name="diff_decode_attention",
    )(page_table.reshape(-1), qkv4, qkv4, qkv4, kc, vc, bp, b0, lam_params, sw)
    return out.reshape(b, d)


def _moba_decode_kernel(pt_ref, q_ref, kn_ref, vn_ref, kc_ref, vc_ref, bp_ref, b0_ref,
                        o_ref, qr_ref, m_ref, l_ref, acc_ref, ks_ref, *, pages_per_block):
    p = pl.program_id(1)
    d = q_ref.shape[-1]
    n_blocks = ks_ref.shape[0]
    blk = p // pages_per_block
    scale = HEAD_DIM ** -0.5
    own = _head_of_lane((N_HEADS, d)) == lax.broadcasted_iota(jnp.int32, (N_HEADS, d), 0)

    @pl.when(p == 0)
    def _():
        qr_ref[...] = jnp.where(own, q_ref[0, 0] * scale, 0.0).astype(BF16)

    k32 = kc_ref[0]
    ksum = jnp.sum(k32, axis=0, keepdims=True)
    s = lax.dot_general(qr_ref[...], k32.astype(BF16), _NT,
                        preferred_element_type=F32) + bp_ref[0]
    vb = vc_ref[0].astype(BF16)

    @pl.when(p % pages_per_block == 0)
    def _():
        m = jnp.max(s, axis=1, keepdims=True)
        pr = jnp.exp(s - m)
        m_ref[blk] = m
        l_ref[blk] = jnp.sum(pr, axis=1, keepdims=True)
        acc_ref[blk] = jnp.dot(pr.astype(BF16), vb, preferred_element_type=F32)
        ks_ref[blk] = ksum

    @pl.when(p % pages_per_block != 0)
    def _():
        m_prev = m_ref[blk]
        m = jnp.maximum(m_prev, jnp.max(s, axis=1, keepdims=True))
        pr = jnp.exp(s - m)
        alpha = jnp.exp(m_prev - m)
        m_ref[blk] = m
        l_ref[blk] = alpha * l_ref[blk] + jnp.sum(pr, axis=1, keepdims=True)
        acc_ref[blk] = alpha * acc_ref[blk] + jnp.dot(pr.astype(BF16), vb, preferred_element_type=F32)
        ks_ref[blk] = ks_ref[blk] + ksum

    @pl.when(p == pl.num_programs(1) - 1)
    def _():
        qh = jnp.where(own, q_ref[0, 0], 0.0)
        gates = [jnp.sum(qh * (ks_ref[n] * (1.0 / MOBA_BLOCK)), axis=1, keepdims=True)
                 for n in range(n_blocks)]
        m_tot = jnp.sum(qh * scale * kn_ref[0, 0], axis=1, keepdims=True) + b0_ref[...]
        s_self = m_tot
        sels = []
        for n in range(n_blocks):
            rank = jnp.zeros((N_HEADS, 1), F32)
            for n2 in range(n_blocks):
                if n2 < n:
                    rank = rank + jnp.where(gates[n2] >= gates[n], 1.0, 0.0)
                elif n2 > n:
                    rank = rank + jnp.where(gates[n2] > gates[n], 1.0, 0.0)
            sel = rank < float(min(MOBA_TOPK, n_blocks))
            sels.append(sel)
            m_tot = jnp.maximum(m_tot, jnp.where(sel, m_ref[n], NEG))
        w_self = jnp.exp(s_self - m_tot)
        l_tot = w_self
        o = w_self * vn_ref[0, 0]
        for n in range(n_blocks):
            w = jnp.where(sels[n], jnp.exp(m_ref[n] - m_tot), 0.0)
            l_tot = l_tot + w * l_ref[n]
            o = o + w * acc_ref[n]
        o = jnp.where(own, o * (1.0 / l_tot), 0.0)
        o_ref[0] = jnp.sum(o, axis=0, keepdims=True).astype(BF16)


def _moba_decode_attention(qkv, cache_k, cache_v, page_table, bias_pages, bias_self):
    _, b, d = qkv.shape
    n_pages = page_table.shape[1]
    n_phys = cache_k.shape[0]
    pages_per_block = MOBA_BLOCK // PAGE_SIZE
    assert n_pages % pages_per_block == 0
    n_blocks = n_pages // pages_per_block
    qkv4 = qkv.reshape(3, b, 1, d)
    kc = cache_k.reshape(n_phys, PAGE_SIZE, d)
    vc = cache_v.reshape(n_phys, PAGE_SIZE, d)

    def page(bi, p, pt):
        return (pt[bi * n_pages + p], 0, 0)

    out = pl.pallas_call(
        functools.partial(_moba_decode_kernel, pages_per_block=pages_per_block),
        grid_spec=pltpu.PrefetchScalarGridSpec(
            num_scalar_prefetch=1,
            grid=(b, n_pages),
            in_specs=[pl.BlockSpec((1, 1, 1, d), lambda bi, p, pt: (0, bi, 0, 0)),
                      pl.BlockSpec((1, 1, 1, d), lambda bi, p, pt: (1, bi, 0, 0)),
                      pl.BlockSpec((1, 1, 1, d), lambda bi, p, pt: (2, bi, 0, 0)),
                      pl.BlockSpec((1, PAGE_SIZE, d), page),
                      pl.BlockSpec((1, PAGE_SIZE, d), page),
                      pl.BlockSpec((1, N_HEADS, PAGE_SIZE), lambda bi, p, pt: (p, 0, 0)),
                      pl.BlockSpec((N_HEADS, 1), lambda bi, p, pt: (0, 0))],
            out_specs=pl.BlockSpec((1, 1, d), lambda bi, p, pt: (bi, 0, 0)),
            scratch_shapes=[pltpu.VMEM((N_HEADS, d), BF16),
                            pltpu.VMEM((n_blocks, N_HEADS, 1), F32),
                            pltpu.VMEM((n_blocks, N_HEADS, 1), F32),
                            pltpu.VMEM((n_blocks, N_HEADS, d), F32),
                            pltpu.VMEM((n_blocks, 1, d), F32)]),
        out_shape=jax.ShapeDtypeStruct((b, 1, d), BF16),
        compiler_params=_params(2),
        name="moba_decode_attention",
    )(page_table.reshape(-1), qkv4, qkv4, qkv4, kc, vc, bias_pages, bias_self)
    return out.reshape(b, d)


def _row_block(m):
    return 512 if m % 512 == 0 else m


def kernel(x_prompt, x_sample, cache_k_diff, cache_v_diff, cache_k_moba, cache_v_moba, state_conv, page_table, rel_bias, norm_mix, norm_ffn, norm_final, w_qkv_diff, lambda_diff, subln_diff, w_o_diff, w_qkv_moba, w_o_moba, w_up, conv_w, conv_b, w_down):
    bp_, s, d = x_prompt.shape
    bs_, ts, _ = x_sample.shape
    assert bp_ == 1 and ts == 1 and d == N_HEADS * HEAD_DIM
    depth = norm_mix.shape[0]
    past_len = page_table.shape[1] * PAGE_SIZE
    dff = conv_w.shape[-1]
    bn_ff = 512 if dff % 512 == 0 else dff

    hp = x_prompt.reshape(s, d)
    hs = x_sample.reshape(bs_, d)
    bm_p, bm_s = _row_block(s), _row_block(bs_)

    bias_diag, bias_sub = _prompt_bias_tiles(rel_bias, ATT_TILE)
    bias_pages, bias_self = _decode_bias_tables(rel_bias, past_len)

    kd_p, vd_p, km_p, vm_p, cv_p = [], [], [], [], []
    kd_s, vd_s, km_s, vm_s, cv_s = [], [], [], [], []
    for i in range(depth):
        slot = i // 2
        if i % 2 == 0:
            li = _lambda_init(i)
            w_qkv = w_qkv_diff[slot].astype(BF16)
            w_o = w_o_diff[slot].astype(BF16)
            qkv_p = _norm_qkv(hp, norm_mix[i], w_qkv, bm=bm_p, bn=1024)
            qkv_s = _norm_qkv(hs, norm_mix[i], w_qkv, bm=bm_s, bn=1024)
            o_p = _diff_prompt_attention(qkv_p, bias_diag, bias_sub, lambda_diff[slot], subln_diff[slot], li)
            o_s = _diff_decode_attention(qkv_s, cache_k_diff[slot], cache_v_diff[slot], page_table,
                                         bias_pages, bias_self, lambda_diff[slot], subln_diff[slot], li)
            kd_p.append(qkv_p[1].reshape(1, s, N_HEADS, 2, DIFF_DK))
            vd_p.append(qkv_p[2].reshape(1, s, N_HEADS, HEAD_DIM))
            kd_s.append(qkv_s[1].reshape(bs_, 1, N_HEADS, 2, DIFF_DK))
            vd_s.append(qkv_s[2].reshape(bs_, 1, N_HEADS, HEAD_DIM))
        else:
            w_qkv = w_qkv_moba[slot].astype(BF16)
            w_o = w_o_moba[slot].astype(BF16)
            qkv_p = _norm_qkv(hp, norm_mix[i], w_qkv, bm=bm_p, bn=1024)
            qkv_s = _norm_qkv(hs, norm_mix[i], w_qkv, bm=bm_s, bn=1024)
            o_p = _moba_prompt_attention(qkv_p, _block_means(qkv_p), bias_diag, bias_sub)
            o_s = _moba_decode_attention(qkv_s, cache_k_moba[slot], cache_v_moba[slot], page_table,
                                         bias_pages, bias_self)
            km_p.append(qkv_p[1].reshape(1, s, N_HEADS, HEAD_DIM))
            vm_p.append(qkv_p[2].reshape(1, s, N_HEADS, HEAD_DIM))
            km_s.append(qkv_s[1].reshape(bs_, 1, N_HEADS, HEAD_DIM))
            vm_s.append(qkv_s[2].reshape(bs_, 1, N_HEADS, HEAD_DIM))
        hp = _matmul_residual(o_p, w_o, hp, bm=bm_p, bn=512)
        hs = _matmul_residual(o_s, w_o, hs, bm=bm_s, bn=512)

        wu = w_up[i].astype(BF16)
        wd = w_down[i].astype(BF16)
        f_p, st_p = _ffn_up_seq(hp, norm_ffn[i], wu, conv_w[i], conv_b[i], bm=bm_p, bn=bn_ff)
        f_s, st_s = _ffn_up_tok(hs, norm_ffn[i], wu, conv_w[i], conv_b[i], state_conv[i], bn=bn_ff)
        hp = _matmul_residual(f_p, wd, hp, bm=bm_p, bn=512)
        hs = _matmul_residual(f_s, wd, hs, bm=bm_s, bn=512)
        cv_p.append(st_p.reshape(1, CONV_W - 1, dff))
        cv_s.append(st_s)

    y_p = _rmsnorm(hp, norm_final, bm=bm_p).reshape(1, s, d)
    y_s = _rmsnorm(hs, norm_final, bm=bm_s).reshape(bs_, 1, d)
    return (y_p, y_s,
            jnp.stack(kd_p), jnp.stack(vd_p), jnp.stack(km_p), jnp.stack(vm_p), jnp.stack(cv_p),
            jnp.stack(kd_s), jnp.stack(vd_s), jnp.stack(km_s), jnp.stack(vm_s), jnp.stack(cv_s))
```

```python
import functools
import math

import jax
import jax.numpy as jnp
from jax import lax
from jax.experimental import pallas as pl
from jax.experimental.pallas import tpu as pltpu

F32 = jnp.float32
BF16 = jnp.bfloat16

N_HEADS = 16
HEAD_DIM = 128
DIFF_DK = HEAD_DIM // 2
MOBA_BLOCK = 256
MOBA_TOPK = 3
PAGE_SIZE = 128
CONV_W = 3
N_BUCKETS = 32
MAX_EXACT = N_BUCKETS // 2
MAX_DISTANCE = 128
EPS = 1e-6
SUBLN_EPS = 1e-5
NEG = -1e30
ATT_TILE = 512
PAGES_PER_STEP = MOBA_BLOCK // PAGE_SIZE
VMEM_LIMIT = 56 * 1024 * 1024

_NT = (((1,), (1,)), ((), ()))


def _lambda_init(i):
    return 0.8 - 0.6 * math.exp(-0.3 * i)


def _params(n_grid_dims):
    return pltpu.CompilerParams(dimension_semantics=("arbitrary",) * n_grid_dims,
                                vmem_limit_bytes=VMEM_LIMIT)


def _norm_rows(x, g):
    r = lax.rsqrt(jnp.mean(x * x, axis=-1, keepdims=True) + EPS)
    return x * r * g


def _norm_matmul_kernel(x_ref, g_ref, w_ref, o_ref, xn_ref):
    @pl.when(pl.program_id(1) == 0)
    def _():
        xn_ref[...] = _norm_rows(x_ref[...], g_ref[...]).astype(BF16)

    o_ref[0] = jnp.dot(xn_ref[...], w_ref[...], preferred_element_type=F32)


def _norm_qkv(x, g, w, *, bm, bn):
    m, d = x.shape
    nb = d // bn
    return pl.pallas_call(
        _norm_matmul_kernel,
        grid=(m // bm, 3 * nb),
        in_specs=[pl.BlockSpec((bm, d), lambda i, j: (i, 0)),
                  pl.BlockSpec((1, d), lambda i, j: (0, 0)),
                  pl.BlockSpec((d, bn), lambda i, j: (0, j))],
        out_specs=pl.BlockSpec((1, bm, bn), lambda i, j: (j // nb, i, j % nb)),
        out_shape=jax.ShapeDtypeStruct((3, m, d), F32),
        scratch_shapes=[pltpu.VMEM((bm, d), BF16)],
        compiler_params=_params(2),
        name="norm_qkv",
    )(x, g.reshape(1, d), w)


def _mm_res_kernel(a_ref, w_ref, r_ref, o_ref):
    o_ref[...] = r_ref[...] + jnp.dot(a_ref[...], w_ref[...], preferred_element_type=F32)


def _matmul_residual(a, w, res, *, bm, bn):
    m, k = a.shape
    n = w.shape[1]
    return pl.pallas_call(
        _mm_res_kernel,
        grid=(m // bm, n // bn),
        in_specs=[pl.BlockSpec((bm, k), lambda i, j: (i, 0)),
                  pl.BlockSpec((k, bn), lambda i, j: (0, j)),
                  pl.BlockSpec((bm, bn), lambda i, j: (i, j))],
        out_specs=pl.BlockSpec((bm, bn), lambda i, j: (i, j)),
        out_shape=jax.ShapeDtypeStruct((m, n), F32),
        compiler_params=_params(2),
        name="matmul_residual",
    )(a, w, res)


def _rmsnorm_kernel(x_ref, g_ref, o_ref):
    o_ref[...] = _norm_rows(x_ref[...], g_ref[...])


def _rmsnorm(x, g, *, bm):
    m, d = x.shape
    return pl.pallas_call(
        _rmsnorm_kernel,
        grid=(m // bm,),
        in_specs=[pl.BlockSpec((bm, d), lambda i: (i, 0)),
                  pl.BlockSpec((1, d), lambda i: (0, 0))],
        out_specs=pl.BlockSpec((bm, d), lambda i: (i, 0)),
        out_shape=jax.ShapeDtypeStruct((m, d), F32),
        compiler_params=_params(1),
        name="final_rmsnorm",
    )(x, g.reshape(1, d))


def _silu_gate(gc, u):
    return gc * (1.0 / (1.0 + jnp.exp(-gc))) * u


def _ffn_up_seq_kernel(x_ref, g_ref, wg_ref, wu_ref, cw_ref, cb_ref, h_ref, st_ref,
                       xn_ref, gbuf_ref, carry_ref, *, bm):
    i = pl.program_id(0)
    j = pl.program_id(1)

    @pl.when(j == 0)
    def _():
        xn_ref[...] = _norm_rows(x_ref[...], g_ref[...]).astype(BF16)

    xn = xn_ref[...]
    g = jnp.dot(xn, wg_ref[...], preferred_element_type=F32)
    u = jnp.dot(xn, wu_ref[...], preferred_element_type=F32)

    @pl.when(i == 0)
    def _():
        gbuf_ref[0:8, :] = jnp.zeros((8, g.shape[1]), F32)

    @pl.when(i > 0)
    def _():
        gbuf_ref[0:8, :] = carry_ref[j]

    gbuf_ref[8:8 + bm, :] = g
    g1 = gbuf_ref[7:7 + bm, :]
    g2 = gbuf_ref[6:6 + bm, :]
    cw = cw_ref[...]
    gc = cb_ref[...] + cw[2:3] * g
    gc = gc + cw[0:1] * g2
    gc = gc + cw[1:2] * g1
    h_ref[...] = _silu_gate(gc, u).astype(BF16)
    last8 = gbuf_ref[bm:bm + 8, :]
    carry_ref[j] = last8
    st_ref[0] = last8


def _ffn_up_seq(x, g, w_up, conv_w, conv_b, *, bm, bn):
    m, d = x.shape
    dff = conv_w.shape[1]
    nj = dff // bn
    h, st = pl.pallas_call(
        functools.partial(_ffn_up_seq_kernel, bm=bm),
        grid=(m // bm, nj),
        in_specs=[pl.BlockSpec((bm, d), lambda i, j: (i, 0)),
                  pl.BlockSpec((1, d), lambda i, j: (0, 0)),
                  pl.BlockSpec((d, bn), lambda i, j: (0, j)),
                  pl.BlockSpec((d, bn), lambda i, j: (0, j + nj)),
                  pl.BlockSpec((CONV_W, bn), lambda i, j: (0, j)),
                  pl.BlockSpec((1, bn), lambda i, j: (0, j))],
        out_specs=[pl.BlockSpec((bm, bn), lambda i, j: (i, j)),
                   pl.BlockSpec((1, 8, bn), lambda i, j: (i, 0, j))],
        out_shape=[jax.ShapeDtypeStruct((m, dff), BF16),
                   jax.ShapeDtypeStruct((m // bm, 8, dff), F32)],
        scratch_shapes=[pltpu.VMEM((bm, d), BF16),
                        pltpu.VMEM((bm + 8, bn), F32),
                        pltpu.VMEM((nj, 8, bn), F32)],
        compiler_params=_params(2),
        name="ffn_up_seq",
    )(x, g.reshape(1, d), w_up, w_up, conv_w, conv_b.reshape(1, dff))
    return h, st[-1, 8 - (CONV_W - 1):]


def _ffn_up_tok_kernel(x_ref, g_ref, wg_ref, wu_ref, cw_ref, cb_ref, p0_ref, p1_ref,
                       h_ref, gout_ref, xn_ref):
    @pl.when(pl.program_id(0) == 0)
    def _():
        xn_ref[...] = _norm_rows(x_ref[...], g_ref[...]).astype(BF16)

    xn = xn_ref[...]
    g = jnp.dot(xn, wg_ref[...], preferred_element_type=F32)
    u = jnp.dot(xn, wu_ref[...], preferred_element_type=F32)
    cw = cw_ref[...]
    gc = cb_ref[...] + cw[2:3] * g
    gc = gc + cw[0:1] * p0_ref[...]
    gc = gc + cw[1:2] * p1_ref[...]
    h_ref[...] = _silu_gate(gc, u).astype(BF16)
    gout_ref[...] = g


def _ffn_up_tok(x, g, w_up, conv_w, conv_b, prev, *, bn):
    m, d = x.shape
    dff = conv_w.shape[1]
    nj = dff // bn
    h, gout = pl.pallas_call(
        _ffn_up_tok_kernel,
        grid=(nj,),
        in_specs=[pl.BlockSpec((m, d), lambda j: (0, 0)),
                  pl.BlockSpec((1, d), lambda j: (0, 0)),
                  pl.BlockSpec((d, bn), lambda j: (0, j)),
                  pl.BlockSpec((d, bn), lambda j: (0, j + nj)),
                  pl.BlockSpec((CONV_W, bn), lambda j: (0, j)),
                  pl.BlockSpec((1, bn), lambda j: (0, j)),
                  pl.BlockSpec((m, bn), lambda j: (0, j)),
                  pl.BlockSpec((m, bn), lambda j: (0, j))],
        out_specs=[pl.BlockSpec((m, bn), lambda j: (0, j)),
                   pl.BlockSpec((m, bn), lambda j: (0, j))],
        out_shape=[jax.ShapeDtypeStruct((m, dff), BF16),
                   jax.ShapeDtypeStruct((m, dff), F32)],
        scratch_shapes=[pltpu.VMEM((m, d), BF16)],
        compiler_params=_params(1),
        name="ffn_up_tok",
    )(x, g.reshape(1, d), w_up, w_up, conv_w, conv_b.reshape(1, dff), prev[:, 0], prev[:, 1])
    return h, jnp.stack([prev[:, 1], gout], axis=1)


def _rel_bucket(dist):
    n = jnp.maximum(dist, 0)
    nf = jnp.maximum(n, MAX_EXACT).astype(F32)
    large = MAX_EXACT + (jnp.log(nf / MAX_EXACT) / math.log(MAX_DISTANCE / MAX_EXACT)
                         * (N_BUCKETS - MAX_EXACT)).astype(jnp.int32)
    large = jnp.minimum(large, N_BUCKETS - 1)
    return jnp.where(n < MAX_EXACT, n, large)


def _prompt_bias_by_distance(rel_bias, t):
    assert t >= MAX_DISTANCE
    far = rel_bias[N_BUCKETS - 1]
    by_dist = rel_bias[_rel_bucket(jnp.arange(2 * t))] - far[None, :]
    return by_dist.T.reshape(N_HEADS, 1, 2 * t).astype(F32)


def _decode_bias_tables(rel_bias, past_len):
    far = rel_bias[N_BUCKETS - 1]
    dist = past_len - jnp.arange(past_len)
    pages = (rel_bias[_rel_bucket(dist)] - far[None, :]).reshape(past_len // PAGE_SIZE, PAGE_SIZE, N_HEADS)
    self_bias = (rel_bias[0] - far).reshape(N_HEADS, 1)
    return jnp.transpose(pages, (0, 2, 1)).astype(F32), self_bias.astype(F32)


def _fill_bias_tiles(bb_ref, bd_ref, bs_ref, t):
    w = bb_ref[0]
    lane = lax.broadcasted_iota(jnp.int32, (1, 2 * t), 1)
    w_causal = jnp.where(lane >= t, pltpu.roll(w, t, 1), NEG)
    rolled = pltpu.roll(jnp.broadcast_to(w_causal, (t, 2 * t)), 0, 1, stride=1, stride_axis=0)
    bd_ref[...] = rolled[:, t:]
    rolled = pltpu.roll(jnp.broadcast_to(w, (t, 2 * t)), 0, 1, stride=1, stride_axis=0)
    bs_ref[...] = rolled[:, t:]


def _diff_lambda(lp, lam_init):
    a = jnp.sum(lp[0:1] * lp[1:2], axis=-1, keepdims=True)
    b = jnp.sum(lp[2:3] * lp[3:4], axis=-1, keepdims=True)
    return jnp.exp(a) - jnp.exp(b) + lam_init


def _flash_first(s, vt, m_ref, l_ref, acc_ref):
    m = jnp.max(s, axis=0, keepdims=True)
    p = jnp.exp(s - m)
    m_ref[...] = m
    l_ref[...] = jnp.sum(p, axis=0, keepdims=True)
    acc_ref[...] = jnp.dot(vt, p.astype(BF16), preferred_element_type=F32)


def _flash_next(s, vt, m_ref, l_ref, acc_ref):
    m_prev = m_ref[...]
    m = jnp.maximum(m_prev, jnp.max(s, axis=0, keepdims=True))
    p = jnp.exp(s - m)
    alpha = jnp.exp(m_prev - m)
    m_ref[...] = m
    l_ref[...] = alpha * l_ref[...] + jnp.sum(p, axis=0, keepdims=True)
    acc_ref[...] = alpha * acc_ref[...] + jnp.dot(vt, p.astype(BF16), preferred_element_type=F32)


def _diff_prompt_kernel(q_ref, k_ref, v_ref, bb_ref, lp_ref, sw_ref, o_ref,
                        kb_ref, vt_ref, bd_ref, bs_ref, m_ref, l_ref, acc_ref, *, t, lam_init):
    i = pl.program_id(1)
    n_tiles = kb_ref.shape[0]

    @pl.when(i == 0)
    def _():
        _fill_bias_tiles(bb_ref, bd_ref, bs_ref, t)

        def fill(c, carry):
            r0 = pl.multiple_of(c * t, t)
            kb_ref[c] = k_ref[0, pl.ds(r0, t), :].astype(BF16)
            vt_ref[c] = v_ref[0, pl.ds(r0, t), :].T.astype(BF16)
            return carry
        lax.fori_loop(0, n_tiles, fill, 0)

    q = q_ref[0] * (DIFF_DK ** -0.5)
    lane = lax.broadcasted_iota(jnp.int32, q.shape, 1)
    q0 = jnp.where(lane < DIFF_DK, q, 0.0)
    q1 = jnp.where(lane >= DIFF_DK, q, 0.0)
    qt = jnp.concatenate([q0.T, q1.T], axis=1).astype(BF16)

    def scores(j):
        return jnp.dot(kb_ref[j], qt, preferred_element_type=F32)

    bd = bd_ref[...]
    _flash_first(scores(i) + jnp.concatenate([bd, bd], axis=1), vt_ref[i], m_ref, l_ref, acc_ref)

    @pl.when(i > 0)
    def _():
        bs = bs_ref[...]
        _flash_next(scores(i - 1) + jnp.concatenate([bs, bs], axis=1), vt_ref[i - 1],
                    m_ref, l_ref, acc_ref)

    def far(j, carry):
        _flash_next(scores(j), vt_ref[j], m_ref, l_ref, acc_ref)
        return carry
    lax.fori_loop(0, i - 1, far, 0)

    o2 = acc_ref[...] * (1.0 / l_ref[...])
    lam = _diff_lambda(lp_ref[...], lam_init)
    o = o2[:, :t] - lam * o2[:, t:]
    o = o * lax.rsqrt(jnp.mean(o * o, axis=0, keepdims=True) + SUBLN_EPS)
    o_ref[...] = (o.T * sw_ref[...] * (1.0 - lam_init)).astype(BF16)


def _diff_prompt_attention(qkv, bias_by_dist, lam_params, subln_w, lam_init):
    _, s, d = qkv.shape
    t = ATT_TILE
    nt = s // t
    return pl.pallas_call(
        functools.partial(_diff_prompt_kernel, t=t, lam_init=lam_init),
        grid=(N_HEADS, nt),
        in_specs=[pl.BlockSpec((1, t, HEAD_DIM), lambda h, i: (0, i, h)),
                  pl.BlockSpec((1, s, HEAD_DIM), lambda h, i: (1, 0, h)),
                  pl.BlockSpec((1, s, HEAD_DIM), lambda h, i: (2, 0, h)),
                  pl.BlockSpec((1, 1, 2 * t), lambda h, i: (h, 0, 0)),
                  pl.BlockSpec((4, DIFF_DK), lambda h, i: (0, 0)),
                  pl.BlockSpec((1, HEAD_DIM), lambda h, i: (0, 0))],
        out_specs=pl.BlockSpec((t, HEAD_DIM), lambda h, i: (i, h)),
        out_shape=jax.ShapeDtypeStruct((s, d), BF16),
        scratch_shapes=[pltpu.VMEM((nt, t, HEAD_DIM), BF16),
                        pltpu.VMEM((nt, HEAD_DIM, t), BF16),
                        pltpu.VMEM((t, t), F32),
                        pltpu.VMEM((t, t), F32),
                        pltpu.VMEM((1, 2 * t), F32),
                        pltpu.VMEM((1, 2 * t), F32),
                        pltpu.VMEM((HEAD_DIM, 2 * t), F32)],
        compiler_params=_params(2),
        name="diff_prompt_attention",
    )(qkv, qkv, qkv, bias_by_dist, lam_params, subln_w.reshape(1, HEAD_DIM))


def _block_mean_kernel(k_ref, o_ref):
    o_ref[0] = jnp.sum(k_ref[0], axis=0, keepdims=True) * (1.0 / MOBA_BLOCK)


def _block_means(qkv):
    _, s, d = qkv.shape
    nb = s // MOBA_BLOCK
    out = pl.pallas_call(
        _block_mean_kernel,
        grid=(nb,),
        in_specs=[pl.BlockSpec((1, MOBA_BLOCK, d), lambda n: (1, n, 0))],
        out_specs=pl.BlockSpec((1, 1, d), lambda n: (n, 0, 0)),
        out_shape=jax.ShapeDtypeStruct((nb, 1, d), F32),
        compiler_params=_params(1),
        name="moba_block_means",
    )(qkv)
    return out.reshape(nb, d)


def _split_bf16(x):
    hi = x.astype(BF16)
    lo = (x - hi.astype(F32)).astype(BF16)
    return hi, lo


def _moba_prompt_kernel(q_ref, k_ref, v_ref, km_ref, bb_ref, o_ref,
                        ka_ref, vt_ref, bd_ref, bs_ref, m_ref, l_ref, acc_ref, *, t):
    i = pl.program_id(1)
    n_tiles = ka_ref.shape[0]
    nb = km_ref.shape[0]
    blocks_per_tile = t // MOBA_BLOCK

    @pl.when(i == 0)
    def _():
        _fill_bias_tiles(bb_ref, bd_ref, bs_ref, t)
        lane = lax.broadcasted_iota(jnp.int32, (t, HEAD_DIM), 1)
        sub = lax.broadcasted_iota(jnp.int32, (t, HEAD_DIM), 0)

        def fill(c, carry):
            r0 = pl.multiple_of(c * t, t)
            ka_ref[c, :, 0:HEAD_DIM] = k_ref[0, pl.ds(r0, t), :].astype(BF16)
            ka_ref[c, :, HEAD_DIM:] = jnp.where(
                lane == c * blocks_per_tile + sub // MOBA_BLOCK, 1.0, 0.0).astype(BF16)
            vt_ref[c] = v_ref[0, pl.ds(r0, t), :].T.astype(BF16)
            return carry
        lax.fori_loop(0, n_tiles, fill, 0)

    qt = q_ref[0].T

    km_hi, km_lo = _split_bf16(km_ref[...])
    qt_hi, qt_lo = _split_bf16(qt)
    gate = (jnp.dot(km_hi, qt_hi, preferred_element_type=F32)
            + jnp.dot(km_hi, qt_lo, preferred_element_type=F32)
            + jnp.dot(km_lo, qt_hi, preferred_element_type=F32))
    row = lax.broadcasted_iota(jnp.int32, (nb, t), 0)
    own = i * blocks_per_tile + lax.broadcasted_iota(jnp.int32, (nb, t), 1) // MOBA_BLOCK
    rowf = row.astype(F32)
    g = jnp.where(row < own, gate, NEG)
    sel = row == own
    for _ in range(MOBA_TOPK):
        mx = jnp.max(g, axis=0, keepdims=True)
        is_max = (g == mx) & (mx > 0.5 * NEG)
        first = jnp.min(jnp.where(is_max, rowf, float(nb)), axis=0, keepdims=True)
        pick = rowf == first
        sel = sel | pick
        g = jnp.where(pick, NEG, g)
    mask_rows = jnp.where(sel, 0.0, NEG)
    qa = jnp.concatenate([qt * (HEAD_DIM ** -0.5), mask_rows,
                          jnp.zeros((HEAD_DIM - nb, t), F32)], axis=0).astype(BF16)

    def scores(j):
        return jnp.dot(ka_ref[j], qa, preferred_element_type=F32)

    _flash_first(scores(i) + bd_ref[...], vt_ref[i], m_ref, l_ref, acc_ref)

    @pl.when(i > 0)
    def _():
        _flash_next(scores(i - 1) + bs_ref[...], vt_ref[i - 1], m_ref, l_ref, acc_ref)

    def far(j, carry):
        _flash_next(scores(j), vt_ref[j], m_ref, l_ref, acc_ref)
        return carry
    lax.fori_loop(0, i - 1, far, 0)

    o = acc_ref[...] * (1.0 / l_ref[...])
    o_ref[...] = o.T.astype(BF16)


def _moba_prompt_attention(qkv, km, bias_by_dist):
    _, s, d = qkv.shape
    t = ATT_TILE
    nt = s // t
    nb = km.shape[0]
    assert nb <= HEAD_DIM and t % MOBA_BLOCK == 0
    return pl.pallas_call(
        functools.partial(_moba_prompt_kernel, t=t),
        grid=(N_HEADS, nt),
        in_specs=[pl.BlockSpec((1, t, HEAD_DIM), lambda h, i: (0, i, h)),
                  pl.BlockSpec((1, s, HEAD_DIM), lambda h, i: (1, 0, h)),
                  pl.BlockSpec((1, s, HEAD_DIM), lambda h, i: (2, 0, h)),
                  pl.BlockSpec((nb, HEAD_DIM), lambda h, i: (0, h)),
                  pl.BlockSpec((1, 1, 2 * t), lambda h, i: (h, 0, 0))],
        out_specs=pl.BlockSpec((t, HEAD_DIM), lambda h, i: (i, h)),
        out_shape=jax.ShapeDtypeStruct((s, d), BF16),
        scratch_shapes=[pltpu.VMEM((nt, t, 2 * HEAD_DIM), BF16),
                        pltpu.VMEM((nt, HEAD_DIM, t), BF16),
                        pltpu.VMEM((t, t), F32),
                        pltpu.VMEM((t, t), F32),
                        pltpu.VMEM((1, t), F32),
                        pltpu.VMEM((1, t), F32),
                        pltpu.VMEM((HEAD_DIM, t), F32)],
        compiler_params=_params(2),
        name="moba_prompt_attention",
    )(qkv, qkv, qkv, km, bias_by_dist)


def _head_rows(page_ref, h):
    return page_ref[0, pl.ds(h, PAGE_SIZE, stride=N_HEADS), :]


def _page_specs(n_pages, d_rows):
    def spec(g):
        return pl.BlockSpec((1, d_rows, HEAD_DIM),
                            lambda bi, st, pt: (pt[bi * n_pages + st * PAGES_PER_STEP + g], 0, 0))
    return [spec(g) for g in range(PAGES_PER_STEP)]


def _diff_decode_kernel(pt_ref, qb_ref, q2_ref, kn_ref, vn_ref, *rest, lam_init):
    kt_refs = rest[:PAGES_PER_STEP]
    v_refs = rest[PAGES_PER_STEP:2 * PAGES_PER_STEP]
    bp_ref, b0_ref, lp_ref, sw_ref, o_ref, m_ref, l_ref, acc_ref = rest[2 * PAGES_PER_STEP:]
    st = pl.program_id(1)
    rows = 2 * N_HEADS
    scale = DIFF_DK ** -0.5

    @pl.when(st == 0)
    def _():
        m_ref[...] = jnp.sum(q2_ref[0] * scale * kn_ref[0], axis=1, keepdims=True) + b0_ref[...]
        l_ref[...] = jnp.ones((rows, 1), F32)
        acc_ref[...] = vn_ref[0]

    qb = qb_ref[0]
    s_pages = []
    for g in range(PAGES_PER_STEP):
        prod = kt_refs[g][0] * qb
        s_pages.append(jnp.sum(prod.reshape(rows, DIFF_DK, PAGE_SIZE), axis=1) * scale + bp_ref[g])
    s = jnp.concatenate(s_pages, axis=1)
    m_prev = m_ref[...]
    m = jnp.maximum(m_prev, jnp.max(s, axis=1, keepdims=True))
    pr = jnp.exp(s - m)
    alpha = jnp.exp(m_prev - m)
    m_ref[...] = m
    l_ref[...] = alpha * l_ref[...] + jnp.sum(pr, axis=1, keepdims=True)
    prb = pr.astype(BF16)
    head_of_row = lax.broadcasted_iota(jnp.int32, (rows, HEAD_DIM), 0) // 2
    pv = jnp.zeros((rows, HEAD_DIM), F32)
    for h in range(N_HEADS):
        oh = None
        for g in range(PAGES_PER_STEP):
            part = jnp.dot(prb[:, g * PAGE_SIZE:(g + 1) * PAGE_SIZE], _head_rows(v_refs[g], h).astype(BF16),
                           preferred_element_type=F32)
            oh = part if oh is None else oh + part
        pv = jnp.where(head_of_row == h, oh, pv)
    acc_ref[...] = alpha * acc_ref[...] + pv

    @pl.when(st == pl.num_programs(1) - 1)
    def _():
        acc_ref[...] = acc_ref[...] * (1.0 / l_ref[...])
        lam = _diff_lambda(lp_ref[...], lam_init)
        od = acc_ref[pl.ds(0, N_HEADS, stride=2), :] - lam * acc_ref[pl.ds(1, N_HEADS, stride=2), :]
        od = od * lax.rsqrt(jnp.mean(od * od, axis=1, keepdims=True) + SUBLN_EPS)
        o_ref[0] = (od * sw_ref[...] * (1.0 - lam_init)).astype(BF16)


def _diff_decode_attention(qkv, cache_kt, cache_v, page_table, bias_pages, bias_self,
                           lam_params, subln_w, lam_init):
    _, b, d = qkv.shape
    n_pages = page_table.shape[1]
    assert n_pages % PAGES_PER_STEP == 0
    rows = 2 * N_HEADS
    q, kn, vn = qkv[0], qkv[1], qkv[2]
    qb = jnp.broadcast_to(q[:, :, None], (b, d, PAGE_SIZE))
    q2 = q.reshape(b, rows, DIFF_DK)
    kn2 = kn.reshape(b, rows, DIFF_DK)
    vn2 = jnp.repeat(vn.reshape(b, N_HEADS, HEAD_DIM), 2, axis=1)
    bp = jnp.repeat(bias_pages, 2, axis=1)
    b0 = jnp.repeat(bias_self, 2, axis=0)

    per_seq = lambda bi, st, pt: (bi, 0, 0)
    const2 = lambda bi, st, pt: (0, 0)
    out = pl.pallas_call(
        functools.partial(_diff_decode_kernel, lam_init=lam_init),
        grid_spec=pltpu.PrefetchScalarGridSpec(
            num_scalar_prefetch=1,
            grid=(b, n_pages // PAGES_PER_STEP),
            in_specs=[pl.BlockSpec((1, d, PAGE_SIZE), per_seq),
                      pl.BlockSpec((1, rows, DIFF_DK), per_seq),
                      pl.BlockSpec((1, rows, DIFF_DK), per_seq),
                      pl.BlockSpec((1, rows, HEAD_DIM), per_seq)]
                     + _page_specs(n_pages, d) + _page_specs(n_pages, d)
                     + [pl.BlockSpec((PAGES_PER_STEP, rows, PAGE_SIZE), lambda bi, st, pt: (st, 0, 0)),
                        pl.BlockSpec((rows, 1), const2),
                        pl.BlockSpec((4, DIFF_DK), const2),
                        pl.BlockSpec((1, HEAD_DIM), const2)],
            out_specs=pl.BlockSpec((1, N_HEADS, HEAD_DIM), per_seq),
            scratch_shapes=[pltpu.VMEM((rows, 1), F32),
                            pltpu.VMEM((rows, 1), F32),
                            pltpu.VMEM((rows, HEAD_DIM), F32)]),
        out_shape=jax.ShapeDtypeStruct((b, N_HEADS, HEAD_DIM), BF16),
        compiler_params=_params(2),
        name="diff_decode_attention",
    )(page_table.reshape(-1), qb, q2, kn2, vn2,
      *([cache_kt] * PAGES_PER_STEP), *([cache_v] * PAGES_PER_STEP),
      bp, b0, lam_params, subln_w.reshape(1, HEAD_DIM))
    return out.reshape(b, d)


def _moba_decode_kernel(pt_ref, q_ref, kn_ref, vn_ref, *rest):
    k_refs = rest[:PAGES_PER_STEP]
    v_refs = rest[PAGES_PER_STEP:2 * PAGES_PER_STEP]
    bp_ref, b0_ref, o_ref, m_ref, l_ref, acc_ref, ks_ref = rest[2 * PAGES_PER_STEP:]
    blk = pl.program_id(1)
    n_blocks = ks_ref.shape[0]
    scale = HEAD_DIM ** -0.5
    head_of_row = lax.broadcasted_iota(jnp.int32, (N_HEADS, HEAD_DIM), 0)

    q = q_ref[0]
    qs = (q * scale).astype(BF16)
    s_pages = []
    ksum = jnp.zeros((N_HEADS, HEAD_DIM), F32)
    for g in range(PAGES_PER_STEP):
        sg = jnp.zeros((N_HEADS, PAGE_SIZE), F32)
        for h in range(N_HEADS):
            kh = _head_rows(k_refs[g], h)
            ksum = jnp.where(head_of_row == h, ksum + jnp.sum(kh, axis=0, keepdims=True), ksum)
            sh = lax.dot_general(qs, kh.astype(BF16), _NT, preferred_element_type=F32)
            sg = jnp.where(head_of_row == h, sh, sg)
        s_pages.append(sg + bp_ref[g])
    s = jnp.concatenate(s_pages, axis=1)
    m = jnp.max(s, axis=1, keepdims=True)
    pr = jnp.exp(s - m)
    prb = pr.astype(BF16)
    pv = jnp.zeros((N_HEADS, HEAD_DIM), F32)
    for h in range(N_HEADS):
        oh = None
        for g in range(PAGES_PER_STEP):
            part = jnp.dot(prb[:, g * PAGE_SIZE:(g + 1) * PAGE_SIZE], _head_rows(v_refs[g], h).astype(BF16),
                           preferred_element_type=F32)
            oh = part if oh is None else oh + part
        pv = jnp.where(head_of_row == h, oh, pv)
    m_ref[blk] = m
    l_ref[blk] = jnp.sum(pr, axis=1, keepdims=True)
    acc_ref[blk] = pv
    ks_ref[blk] = ksum

    @pl.when(blk == n_blocks - 1)
    def _():
        gates = [jnp.sum(q * (ks_ref[n] * (1.0 / MOBA_BLOCK)), axis=1, keepdims=True)
                 for n in range(n_blocks)]
        s_self = jnp.sum(q * scale * kn_ref[0], axis=1, keepdims=True) + b0_ref[...]
        m_tot = s_self
        sels = []
        for n in range(n_blocks):
            rank = jnp.zeros((N_HEADS, 1), F32)
            for n2 in range(n_blocks):
                if n2 < n:
                    rank = rank + jnp.where(gates[n2] >= gates[n], 1.0, 0.0)
                elif n2 > n:
                    rank = rank + jnp.where(gates[n2] > gates[n], 1.0, 0.0)
            sel = rank < float(min(MOBA_TOPK, n_blocks))
            sels.append(sel)
            m_tot = jnp.maximum(m_tot, jnp.where(sel, m_ref[n], NEG))
        w_self = jnp.exp(s_self - m_tot)
        l_tot = w_self
        o = w_self * vn_ref[0]
        for n in range(n_blocks):
            w = jnp.where(sels[n], jnp.exp(m_ref[n] - m_tot), 0.0)
            l_tot = l_tot + w * l_ref[n]
            o = o + w * acc_ref[n]
        o_ref[0] = (o * (1.0 / l_tot)).astype(BF16)


def _moba_decode_attention(qkv, cache_k, cache_v, page_table, bias_pages, bias_self):
    _, b, d = qkv.shape
    n_pages = page_table.shape[1]
    assert n_pages % PAGES_PER_STEP == 0
    n_blocks = n_pages // PAGES_PER_STEP
    q3 = qkv.reshape(3, b, N_HEADS, HEAD_DIM)

    def qkv_spec(which):
        return pl.BlockSpec((None, 1, N_HEADS, HEAD_DIM), lambda bi, st, pt: (which, bi, 0, 0))

    out = pl.pallas_call(
        _moba_decode_kernel,
        grid_spec=pltpu.PrefetchScalarGridSpec(
            num_scalar_prefetch=1,
            grid=(b, n_blocks),
            in_specs=[qkv_spec(0), qkv_spec(1), qkv_spec(2)]
                     + _page_specs(n_pages, d) + _page_specs(n_pages, d)
                     + [pl.BlockSpec((PAGES_PER_STEP, N_HEADS, PAGE_SIZE), lambda bi, st, pt: (st, 0, 0)),
                        pl.BlockSpec((N_HEADS, 1), lambda bi, st, pt: (0, 0))],
            out_specs=pl.BlockSpec((1, N_HEADS, HEAD_DIM), lambda bi, st, pt: (bi, 0, 0)),
            scratch_shapes=[pltpu.VMEM((n_blocks, N_HEADS, 1), F32),
                            pltpu.VMEM((n_blocks, N_HEADS, 1), F32),
                            pltpu.VMEM((n_blocks, N_HEADS, HEAD_DIM), F32),
                            pltpu.VMEM((n_blocks, N_HEADS, HEAD_DIM), F32)]),
        out_shape=jax.ShapeDtypeStruct((b, N_HEADS, HEAD_DIM), BF16),
        compiler_params=_params(2),
        name="moba_decode_attention",
    )(page_table.reshape(-1), q3, q3, q3,
      *([cache_k] * PAGES_PER_STEP), *([cache_v] * PAGES_PER_STEP), bias_pages, bias_self)
    return out.reshape(b, d)


def _pages_pos_head(cache):
    n_phys = cache.shape[0]
    return cache.reshape(n_phys, PAGE_SIZE * N_HEADS, HEAD_DIM)


def _pages_key_major(cache):
    n_phys = cache.shape[0]
    return jnp.transpose(cache, (0, 2, 3, 4, 1)).reshape(n_phys, N_HEADS * HEAD_DIM, PAGE_SIZE)


def _row_block(m):
    return 512 if m % 512 == 0 else m


def kernel(x_prompt, x_sample, cache_k_diff, cache_v_diff, cache_k_moba, cache_v_moba, state_conv, page_table, rel_bias, norm_mix, norm_ffn, norm_final, w_qkv_diff, lambda_diff, subln_diff, w_o_diff, w_qkv_moba, w_o_moba, w_up, conv_w, conv_b, w_down):
    bp_, s, d = x_prompt.shape
    bs_, ts, _ = x_sample.shape
    assert bp_ == 1 and ts == 1 and d == N_HEADS * HEAD_DIM
    depth = norm_mix.shape[0]
    past_len = page_table.shape[1] * PAGE_SIZE
    dff = conv_w.shape[-1]
    bn_ff = 512 if dff % 512 == 0 else dff

    hp = x_prompt.reshape(s, d)
    hs = x_sample.reshape(bs_, d)
    bm_p, bm_s = _row_block(s), _row_block(bs_)

    bias_by_dist = _prompt_bias_by_distance(rel_bias, ATT_TILE)
    bias_pages, bias_self = _decode_bias_tables(rel_bias, past_len)

    kd_p, vd_p, km_p, vm_p, cv_p = [], [], [], [], []
    kd_s, vd_s, km_s, vm_s, cv_s = [], [], [], [], []
    for i in range(depth):
        slot = i // 2
        if i % 2 == 0:
            li = _lambda_init(i)
            w_qkv = w_qkv_diff[slot].astype(BF16)
            w_o = w_o_diff[slot].astype(BF16)
            qkv_p = _norm_qkv(hp, norm_mix[i], w_qkv, bm=bm_p, bn=1024)
            qkv_s = _norm_qkv(hs, norm_mix[i], w_qkv, bm=bm_s, bn=1024)
            o_p = _diff_prompt_attention(qkv_p, bias_by_dist, lambda_diff[slot], subln_diff[slot], li)
            o_s = _diff_decode_attention(qkv_s, _pages_key_major(cache_k_diff[slot]),
                                         _pages_pos_head(cache_v_diff[slot]), page_table,
                                         bias_pages, bias_self, lambda_diff[slot], subln_diff[slot], li)
            kd_p.append(qkv_p[1].reshape(1, s, N_HEADS, 2, DIFF_DK))
            vd_p.append(qkv_p[2].reshape(1, s, N_HEADS, HEAD_DIM))
            kd_s.append(qkv_s[1].reshape(bs_, 1, N_HEADS, 2, DIFF_DK))
            vd_s.append(qkv_s[2].reshape(bs_, 1, N_HEADS, HEAD_DIM))
        else:
            w_qkv = w_qkv_moba[slot].astype(BF16)
            w_o = w_o_moba[slot].astype(BF16)
            qkv_p = _norm_qkv(hp, norm_mix[i], w_qkv, bm=bm_p, bn=1024)
            qkv_s = _norm_qkv(hs, norm_mix[i], w_qkv, bm=bm_s, bn=1024)
            o_p = _moba_prompt_attention(qkv_p, _block_means(qkv_p), bias_by_dist)
            o_s = _moba_decode_attention(qkv_s, _pages_pos_head(cache_k_moba[slot]),
                                         _pages_pos_head(cache_v_moba[slot]), page_table,
                                         bias_pages, bias_self)
            km_p.append(qkv_p[1].reshape(1, s, N_HEADS, HEAD_DIM))
            vm_p.append(qkv_p[2].reshape(1, s, N_HEADS, HEAD_DIM))
            km_s.append(qkv_s[1].reshape(bs_, 1, N_HEADS, HEAD_DIM))
            vm_s.append(qkv_s[2].reshape(bs_, 1, N_HEADS, HEAD_DIM))
        hp = _matmul_residual(o_p, w_o, hp, bm=bm_p, bn=512)
        hs = _matmul_residual(o_s, w_o, hs, bm=bm_s, bn=512)

        wu = w_up[i].astype(BF16)
        wd = w_down[i].astype(BF16)
        f_p, st_p = _ffn_up_seq(hp, norm_ffn[i], wu, conv_w[i], conv_b[i], bm=bm_p, bn=bn_ff)
        f_s, st_s = _ffn_up_tok(hs, norm_ffn[i], wu, conv_w[i], conv_b[i], state_conv[i], bn=bn_ff)
        hp = _matmul_residual(f_p, wd, hp, bm=bm_p, bn=512)
        hs = _matmul_residual(f_s, wd, hs, bm=bm_s, bn=512)
        cv_p.append(st_p.reshape(1, CONV_W - 1, dff))
        cv_s.append(st_s)

    y_p = _rmsnorm(hp, norm_final, bm=bm_p).reshape(1, s, d)
    y_s = _rmsnorm(hs, norm_final, bm=bm_s).reshape(bs_, 1, d)
    return (y_p, y_s,
            jnp.stack(kd_p), jnp.stack(vd_p), jnp.stack(km_p), jnp.stack(vm_p), jnp.stack(cv_p),
            jnp.stack(kd_s), jnp.stack(vd_s), jnp.stack(km_s), jnp.stack(vm_s), jnp.stack(cv_s))
```

```python
import functools
import math

import jax
import jax.numpy as jnp
from jax import lax
from jax.experimental import pallas as pl
from jax.experimental.pallas import tpu as pltpu

F32 = jnp.float32
BF16 = jnp.bfloat16

N_HEADS = 16
HEAD_DIM = 128
DIFF_DK = HEAD_DIM // 2
MOBA_BLOCK = 256
MOBA_TOPK = 3
PAGE_SIZE = 128
CONV_W = 3
N_BUCKETS = 32
MAX_EXACT = N_BUCKETS // 2
MAX_DISTANCE = 128
EPS = 1e-6
SUBLN_EPS = 1e-5
NEG = -1e30
LOG2E = 1.4426950408889634
ATT_TILE = 512
PAGES_PER_BLOCK = MOBA_BLOCK // PAGE_SIZE
BLOCKS_PER_STEP = 2
PAGES_PER_STEP = PAGES_PER_BLOCK * BLOCKS_PER_STEP
VMEM_LIMIT = 56 * 1024 * 1024

_NT = (((1,), (1,)), ((), ()))


def _lambda_init(i):
    return 0.8 - 0.6 * math.exp(-0.3 * i)


def _params(n_grid_dims):
    return pltpu.CompilerParams(dimension_semantics=("arbitrary",) * n_grid_dims,
                                vmem_limit_bytes=VMEM_LIMIT)


def _norm_rows(x, g):
    r = lax.rsqrt(jnp.mean(x * x, axis=-1, keepdims=True) + EPS)
    return x * r * g


def _norm_matmul_kernel(x_ref, g_ref, w_ref, o_ref, xn_ref):
    @pl.when(pl.program_id(1) == 0)
    def _():
        xn_ref[...] = _norm_rows(x_ref[...], g_ref[...]).astype(BF16)

    o_ref[0] = jnp.dot(xn_ref[...], w_ref[...], preferred_element_type=F32)


def _norm_qkv(x, g, w, *, bm, bn):
    m, d = x.shape
    nb = d // bn
    return pl.pallas_call(
        _norm_matmul_kernel,
        grid=(m // bm, 3 * nb),
        in_specs=[pl.BlockSpec((bm, d), lambda i, j: (i, 0)),
                  pl.BlockSpec((1, d), lambda i, j: (0, 0)),
                  pl.BlockSpec((d, bn), lambda i, j: (0, j))],
        out_specs=pl.BlockSpec((1, bm, bn), lambda i, j: (j // nb, i, j % nb)),
        out_shape=jax.ShapeDtypeStruct((3, m, d), F32),
        scratch_shapes=[pltpu.VMEM((bm, d), BF16)],
        compiler_params=_params(2),
        name="norm_qkv",
    )(x, g.reshape(1, d), w)


def _mm_res_kernel(a_ref, w_ref, r_ref, o_ref):
    o_ref[...] = r_ref[...] + jnp.dot(a_ref[...], w_ref[...], preferred_element_type=F32)


def _matmul_residual(a, w, res, *, bm, bn):
    m, k = a.shape
    n = w.shape[1]
    return pl.pallas_call(
        _mm_res_kernel,
        grid=(m // bm, n // bn),
        in_specs=[pl.BlockSpec((bm, k), lambda i, j: (i, 0)),
                  pl.BlockSpec((k, bn), lambda i, j: (0, j)),
                  pl.BlockSpec((bm, bn), lambda i, j: (i, j))],
        out_specs=pl.BlockSpec((bm, bn), lambda i, j: (i, j)),
        out_shape=jax.ShapeDtypeStruct((m, n), F32),
        compiler_params=_params(2),
        name="matmul_residual",
    )(a, w, res)


def _rmsnorm_kernel(x_ref, g_ref, o_ref):
    o_ref[...] = _norm_rows(x_ref[...], g_ref[...])


def _rmsnorm(x, g, *, bm):
    m, d = x.shape
    return pl.pallas_call(
        _rmsnorm_kernel,
        grid=(m // bm,),
        in_specs=[pl.BlockSpec((bm, d), lambda i: (i, 0)),
                  pl.BlockSpec((1, d), lambda i: (0, 0))],
        out_specs=pl.BlockSpec((bm, d), lambda i: (i, 0)),
        out_shape=jax.ShapeDtypeStruct((m, d), F32),
        compiler_params=_params(1),
        name="final_rmsnorm",
    )(x, g.reshape(1, d))


def _silu_gate(gc, u):
    return gc * (1.0 / (1.0 + jnp.exp(-gc))) * u


def _ffn_up_seq_kernel(x_ref, g_ref, wg_ref, wu_ref, cw_ref, cb_ref, h_ref, st_ref,
                       xn_ref, gbuf_ref, carry_ref, *, bm):
    i = pl.program_id(0)
    j = pl.program_id(1)

    @pl.when(j == 0)
    def _():
        xn_ref[...] = _norm_rows(x_ref[...], g_ref[...]).astype(BF16)

    xn = xn_ref[...]
    g = jnp.dot(xn, wg_ref[...], preferred_element_type=F32)
    u = jnp.dot(xn, wu_ref[...], preferred_element_type=F32)

    @pl.when(i == 0)
    def _():
        gbuf_ref[0:8, :] = jnp.zeros((8, g.shape[1]), F32)

    @pl.when(i > 0)
    def _():
        gbuf_ref[0:8, :] = carry_ref[j]

    gbuf_ref[8:8 + bm, :] = g
    g1 = gbuf_ref[7:7 + bm, :]
    g2 = gbuf_ref[6:6 + bm, :]
    cw = cw_ref[...]
    gc = cb_ref[...] + cw[2:3] * g
    gc = gc + cw[0:1] * g2
    gc = gc + cw[1:2] * g1
    h_ref[...] = _silu_gate(gc, u).astype(BF16)
    last8 = gbuf_ref[bm:bm + 8, :]
    carry_ref[j] = last8
    st_ref[0] = last8


def _ffn_up_seq(x, g, w_up, conv_w, conv_b, *, bm, bn):
    m, d = x.shape
    dff = conv_w.shape[1]
    nj = dff // bn
    h, st = pl.pallas_call(
        functools.partial(_ffn_up_seq_kernel, bm=bm),
        grid=(m // bm, nj),
        in_specs=[pl.BlockSpec((bm, d), lambda i, j: (i, 0)),
                  pl.BlockSpec((1, d), lambda i, j: (0, 0)),
                  pl.BlockSpec((d, bn), lambda i, j: (0, j)),
                  pl.BlockSpec((d, bn), lambda i, j: (0, j + nj)),
                  pl.BlockSpec((CONV_W, bn), lambda i, j: (0, j)),
                  pl.BlockSpec((1, bn), lambda i, j: (0, j))],
        out_specs=[pl.BlockSpec((bm, bn), lambda i, j: (i, j)),
                   pl.BlockSpec((1, 8, bn), lambda i, j: (i, 0, j))],
        out_shape=[jax.ShapeDtypeStruct((m, dff), BF16),
                   jax.ShapeDtypeStruct((m // bm, 8, dff), F32)],
        scratch_shapes=[pltpu.VMEM((bm, d), BF16),
                        pltpu.VMEM((bm + 8, bn), F32),
                        pltpu.VMEM((nj, 8, bn), F32)],
        compiler_params=_params(2),
        name="ffn_up_seq",
    )(x, g.reshape(1, d), w_up, w_up, conv_w, conv_b.reshape(1, dff))
    return h, st[-1, 8 - (CONV_W - 1):]


def _ffn_up_tok_kernel(x_ref, g_ref, wg_ref, wu_ref, cw_ref, cb_ref, p0_ref, p1_ref,
                       h_ref, gout_ref, xn_ref):
    @pl.when(pl.program_id(0) == 0)
    def _():
        xn_ref[...] = _norm_rows(x_ref[...], g_ref[...]).astype(BF16)

    xn = xn_ref[...]
    g = jnp.dot(xn, wg_ref[...], preferred_element_type=F32)
    u = jnp.dot(xn, wu_ref[...], preferred_element_type=F32)
    cw = cw_ref[...]
    gc = cb_ref[...] + cw[2:3] * g
    gc = gc + cw[0:1] * p0_ref[...]
    gc = gc + cw[1:2] * p1_ref[...]
    h_ref[...] = _silu_gate(gc, u).astype(BF16)
    gout_ref[...] = g


def _ffn_up_tok(x, g, w_up, conv_w, conv_b, prev, *, bn):
    m, d = x.shape
    dff = conv_w.shape[1]
    nj = dff // bn
    h, gout = pl.pallas_call(
        _ffn_up_tok_kernel,
        grid=(nj,),
        in_specs=[pl.BlockSpec((m, d), lambda j: (0, 0)),
                  pl.BlockSpec((1, d), lambda j: (0, 0)),
                  pl.BlockSpec((d, bn), lambda j: (0, j)),
                  pl.BlockSpec((d, bn), lambda j: (0, j + nj)),
                  pl.BlockSpec((CONV_W, bn), lambda j: (0, j)),
                  pl.BlockSpec((1, bn), lambda j: (0, j)),
                  pl.BlockSpec((m, bn), lambda j: (0, j)),
                  pl.BlockSpec((m, bn), lambda j: (0, j))],
        out_specs=[pl.BlockSpec((m, bn), lambda j: (0, j)),
                   pl.BlockSpec((m, bn), lambda j: (0, j))],
        out_shape=[jax.ShapeDtypeStruct((m, dff), BF16),
                   jax.ShapeDtypeStruct((m, dff), F32)],
        scratch_shapes=[pltpu.VMEM((m, d), BF16)],
        compiler_params=_params(1),
        name="ffn_up_tok",
    )(x, g.reshape(1, d), w_up, w_up, conv_w, conv_b.reshape(1, dff), prev[:, 0], prev[:, 1])
    return h, jnp.stack([prev[:, 1], gout], axis=1)


def _rel_bucket(dist):
    n = jnp.maximum(dist, 0)
    nf = jnp.maximum(n, MAX_EXACT).astype(F32)
    large = MAX_EXACT + (jnp.log(nf / MAX_EXACT) / math.log(MAX_DISTANCE / MAX_EXACT)
                         * (N_BUCKETS - MAX_EXACT)).astype(jnp.int32)
    large = jnp.minimum(large, N_BUCKETS - 1)
    return jnp.where(n < MAX_EXACT, n, large)


def _prompt_bias_by_distance(rel_bias, t):
    assert t >= MAX_DISTANCE
    far = rel_bias[N_BUCKETS - 1]
    by_dist = rel_bias[_rel_bucket(jnp.arange(2 * t))] - far[None, :]
    return by_dist.T.reshape(N_HEADS, 1, 2 * t).astype(F32)


def _decode_bias_tables(rel_bias, past_len):
    far = rel_bias[N_BUCKETS - 1]
    dist = past_len - jnp.arange(past_len)
    pages = (rel_bias[_rel_bucket(dist)] - far[None, :]).reshape(past_len // PAGE_SIZE, PAGE_SIZE, N_HEADS)
    self_bias = (rel_bias[0] - far).reshape(N_HEADS, 1)
    return jnp.transpose(pages, (0, 2, 1)).astype(F32), self_bias.astype(F32)


def _fill_bias_tiles(bb_ref, bd_ref, bs_ref, t):
    w = bb_ref[0] * LOG2E
    lane = lax.broadcasted_iota(jnp.int32, (1, 2 * t), 1)
    w_causal = jnp.where(lane >= t, pltpu.roll(w, t, 1), NEG)
    rolled = pltpu.roll(jnp.broadcast_to(w_causal, (t, 2 * t)), 0, 1, stride=1, stride_axis=0)
    bd_ref[...] = rolled[:, t:]
    rolled = pltpu.roll(jnp.broadcast_to(w, (t, 2 * t)), 0, 1, stride=1, stride_axis=0)
    bs_ref[...] = rolled[:, t:]


def _diff_lambda(lp, lam_init):
    a = jnp.sum(lp[0:1] * lp[1:2], axis=-1, keepdims=True)
    b = jnp.sum(lp[2:3] * lp[3:4], axis=-1, keepdims=True)
    return jnp.exp(a) - jnp.exp(b) + lam_init


ONES_ROWS = 16


def _values_t_with_ones(v):
    return jnp.concatenate([v.T, jnp.ones((ONES_ROWS, v.shape[0]), F32)], axis=0).astype(BF16)


def _flash_first(s, vt, m_ref, acc_ref):
    m = jnp.max(s, axis=0, keepdims=True)
    p = jnp.exp2(s - m)
    m_ref[...] = m
    acc_ref[...] = jnp.dot(vt, p.astype(BF16), preferred_element_type=F32)


def _flash_next(s, vt, m_ref, acc_ref):
    m_prev = m_ref[...]
    m = jnp.maximum(m_prev, jnp.max(s, axis=0, keepdims=True))
    p = jnp.exp2(s - m)
    alpha = jnp.exp2(m_prev - m)
    m_ref[...] = m
    acc_ref[...] = alpha * acc_ref[...] + jnp.dot(vt, p.astype(BF16), preferred_element_type=F32)


def _normalized(acc_ref):
    return acc_ref[0:HEAD_DIM, :] * (1.0 / acc_ref[HEAD_DIM:HEAD_DIM + 1, :])


def _causal_key_tiles(i, scores, vt_ref, bias_diag, bias_sub, chain_a, chain_b):
    _flash_first(scores(i) + bias_diag(), vt_ref[i], *chain_a)

    @pl.when(i > 0)
    def _():
        _flash_first(scores(i - 1) + bias_sub(), vt_ref[i - 1], *chain_b)

    def softmax_step(s, m_ref):
        m_prev = m_ref[...]
        m = jnp.maximum(m_prev, jnp.max(s, axis=0, keepdims=True))
        p = jnp.exp2(s - m)
        alpha = jnp.exp2(m_prev - m)
        m_ref[...] = m
        return p.astype(BF16), alpha

    def pair(jj, carry):
        s_a = scores(2 * jj)
        s_b = scores(2 * jj + 1)
        p_a, alpha_a = softmax_step(s_a, chain_a[0])
        pv_a = jnp.dot(vt_ref[2 * jj], p_a, preferred_element_type=F32)
        p_b, alpha_b = softmax_step(s_b, chain_b[0])
        pv_b = jnp.dot(vt_ref[2 * jj + 1], p_b, preferred_element_type=F32)
        chain_a[1][...] = alpha_a * chain_a[1][...] + pv_a
        chain_b[1][...] = alpha_b * chain_b[1][...] + pv_b
        return carry
    lax.fori_loop(0, (i - 1) // 2, pair, 0)

    @pl.when(jnp.logical_and(i >= 2, i % 2 == 0))
    def _():
        _flash_next(scores(i - 2), vt_ref[i - 2], *chain_a)

    @pl.when(i > 0)
    def _():
        (ma_ref, acca_ref), (mb_ref, accb_ref) = chain_a, chain_b
        ma, mb = ma_ref[...], mb_ref[...]
        m = jnp.maximum(ma, mb)
        wa, wb = jnp.exp2(ma - m), jnp.exp2(mb - m)
        ma_ref[...] = m
        acca_ref[...] = wa * acca_ref[...] + wb * accb_ref[...]


def _softmax_chain_scratch(width):
    return [pltpu.VMEM((1, width), F32), pltpu.VMEM((HEAD_DIM + ONES_ROWS, width), F32)]


def _diff_prompt_kernel(q_ref, k_ref, v_ref, bb_ref, lp_ref, sw_ref, o_ref,
                        kb_ref, vt_ref, bd_ref, bs_ref, m_ref, acc_ref, m2_ref, acc2_ref, *, t, lam_init):
    i = pl.program_id(1)
    n_tiles = kb_ref.shape[0]

    @pl.when(i == 0)
    def _():
        _fill_bias_tiles(bb_ref, bd_ref, bs_ref, t)

        def fill(c, carry):
            r0 = pl.multiple_of(c * t, t)
            kb_ref[c] = k_ref[0, pl.ds(r0, t), :].astype(BF16)
            vt_ref[c] = _values_t_with_ones(v_ref[0, pl.ds(r0, t), :])
            return carry
        lax.fori_loop(0, n_tiles, fill, 0)

    q = q_ref[0] * (DIFF_DK ** -0.5 * LOG2E)
    lane = lax.broadcasted_iota(jnp.int32, q.shape, 1)
    q0 = jnp.where(lane < DIFF_DK, q, 0.0)
    q1 = jnp.where(lane >= DIFF_DK, q, 0.0)
    qt = jnp.concatenate([q0.T, q1.T], axis=1).astype(BF16)

    def scores(j):
        return jnp.dot(kb_ref[j], qt, preferred_element_type=F32)

    def both_halves(b_ref):
        return lambda: jnp.concatenate([b_ref[...], b_ref[...]], axis=1)

    _causal_key_tiles(i, scores, vt_ref, both_halves(bd_ref), both_halves(bs_ref),
                      (m_ref, acc_ref), (m2_ref, acc2_ref))

    o2 = _normalized(acc_ref)
    lam = _diff_lambda(lp_ref[...], lam_init)
    o = o2[:, :t] - lam * o2[:, t:]
    o = o * lax.rsqrt(jnp.mean(o * o, axis=0, keepdims=True) + SUBLN_EPS)
    o_ref[...] = (o.T * sw_ref[...] * (1.0 - lam_init)).astype(BF16)


def _diff_prompt_attention(qkv, bias_by_dist, lam_params, subln_w, lam_init):
    _, s, d = qkv.shape
    t = ATT_TILE
    nt = s // t
    return pl.pallas_call(
        functools.partial(_diff_prompt_kernel, t=t, lam_init=lam_init),
        grid=(N_HEADS, nt),
        in_specs=[pl.BlockSpec((1, t, HEAD_DIM), lambda h, i: (0, i, h)),
                  pl.BlockSpec((1, s, HEAD_DIM), lambda h, i: (1, 0, h)),
                  pl.BlockSpec((1, s, HEAD_DIM), lambda h, i: (2, 0, h)),
                  pl.BlockSpec((1, 1, 2 * t), lambda h, i: (h, 0, 0)),
                  pl.BlockSpec((4, DIFF_DK), lambda h, i: (0, 0)),
                  pl.BlockSpec((1, HEAD_DIM), lambda h, i: (0, 0))],
        out_specs=pl.BlockSpec((t, HEAD_DIM), lambda h, i: (i, h)),
        out_shape=jax.ShapeDtypeStruct((s, d), BF16),
        scratch_shapes=[pltpu.VMEM((nt, t, HEAD_DIM), BF16),
                        pltpu.VMEM((nt, HEAD_DIM + ONES_ROWS, t), BF16),
                        pltpu.VMEM((t, t), F32),
                        pltpu.VMEM((t, t), F32)]
                       + 2 * _softmax_chain_scratch(2 * t),
        compiler_params=_params(2),
        name="diff_prompt_attention",
    )(qkv, qkv, qkv, bias_by_dist, lam_params, subln_w.reshape(1, HEAD_DIM))


def _block_mean_kernel(k_ref, o_ref):
    o_ref[0] = jnp.sum(k_ref[0], axis=0, keepdims=True) * (1.0 / MOBA_BLOCK)


def _block_means(qkv):
    _, s, d = qkv.shape
    nb = s // MOBA_BLOCK
    out = pl.pallas_call(
        _block_mean_kernel,
        grid=(nb,),
        in_specs=[pl.BlockSpec((1, MOBA_BLOCK, d), lambda n: (1, n, 0))],
        out_specs=pl.BlockSpec((1, 1, d), lambda n: (n, 0, 0)),
        out_shape=jax.ShapeDtypeStruct((nb, 1, d), F32),
        compiler_params=_params(1),
        name="moba_block_means",
    )(qkv)
    return out.reshape(nb, d)


def _split_bf16(x):
    hi = x.astype(BF16)
    lo = (x - hi.astype(F32)).astype(BF16)
    return hi, lo


def _moba_prompt_kernel(q_ref, k_ref, v_ref, km_ref, bb_ref, o_ref,
                        ka_ref, vt_ref, bd_ref, bs_ref, m_ref, acc_ref, m2_ref, acc2_ref, *, t):
    i = pl.program_id(1)
    n_tiles = ka_ref.shape[0]
    nb = km_ref.shape[0]
    blocks_per_tile = t // MOBA_BLOCK

    @pl.when(i == 0)
    def _():
        _fill_bias_tiles(bb_ref, bd_ref, bs_ref, t)
        lane = lax.broadcasted_iota(jnp.int32, (t, HEAD_DIM), 1)
        sub = lax.broadcasted_iota(jnp.int32, (t, HEAD_DIM), 0)

        def fill(c, carry):
            r0 = pl.multiple_of(c * t, t)
            ka_ref[c, :, 0:HEAD_DIM] = k_ref[0, pl.ds(r0, t), :].astype(BF16)
            ka_ref[c, :, HEAD_DIM:] = jnp.where(
                lane == c * blocks_per_tile + sub // MOBA_BLOCK, 1.0, 0.0).astype(BF16)
            vt_ref[c] = _values_t_with_ones(v_ref[0, pl.ds(r0, t), :])
            return carry
        lax.fori_loop(0, n_tiles, fill, 0)

    qt = q_ref[0].T

    km_hi, km_lo = _split_bf16(km_ref[...])
    qt_hi, qt_lo = _split_bf16(qt)
    gate = (jnp.dot(km_hi, qt_hi, preferred_element_type=F32)
            + jnp.dot(km_hi, qt_lo, preferred_element_type=F32)
            + jnp.dot(km_lo, qt_hi, preferred_element_type=F32))
    row = lax.broadcasted_iota(jnp.int32, (nb, t), 0)
    own = i * blocks_per_tile + lax.broadcasted_iota(jnp.int32, (nb, t), 1) // MOBA_BLOCK
    rowf = row.astype(F32)
    g = jnp.where(row < own, gate, NEG)
    sel = row == own
    for _ in range(MOBA_TOPK):
        mx = jnp.max(g, axis=0, keepdims=True)
        is_max = (g == mx) & (mx > 0.5 * NEG)
        first = jnp.min(jnp.where(is_max, rowf, float(nb)), axis=0, keepdims=True)
        pick = rowf == first
        sel = sel | pick
        g = jnp.where(pick, NEG, g)
    mask_rows = jnp.where(sel, 0.0, NEG)
    qa = jnp.concatenate([qt * (HEAD_DIM ** -0.5 * LOG2E), mask_rows,
                          jnp.zeros((HEAD_DIM - nb, t), F32)], axis=0).astype(BF16)

    def scores(j):
        return jnp.dot(ka_ref[j], qa, preferred_element_type=F32)

    _causal_key_tiles(i, scores, vt_ref, lambda: bd_ref[...], lambda: bs_ref[...],
                      (m_ref, acc_ref), (m2_ref, acc2_ref))

    o_ref[...] = _normalized(acc_ref).T.astype(BF16)


def _moba_prompt_attention(qkv, km, bias_by_dist):
    _, s, d = qkv.shape
    t = ATT_TILE
    nt = s // t
    nb = km.shape[0]
    assert nb <= HEAD_DIM and t % MOBA_BLOCK == 0
    return pl.pallas_call(
        functools.partial(_moba_prompt_kernel, t=t),
        grid=(N_HEADS, nt),
        in_specs=[pl.BlockSpec((1, t, HEAD_DIM), lambda h, i: (0, i, h)),
                  pl.BlockSpec((1, s, HEAD_DIM), lambda h, i: (1, 0, h)),
                  pl.BlockSpec((1, s, HEAD_DIM), lambda h, i: (2, 0, h)),
                  pl.BlockSpec((nb, HEAD_DIM), lambda h, i: (0, h)),
                  pl.BlockSpec((1, 1, 2 * t), lambda h, i: (h, 0, 0))],
        out_specs=pl.BlockSpec((t, HEAD_DIM), lambda h, i: (i, h)),
        out_shape=jax.ShapeDtypeStruct((s, d), BF16),
        scratch_shapes=[pltpu.VMEM((nt, t, 2 * HEAD_DIM), BF16),
                        pltpu.VMEM((nt, HEAD_DIM + ONES_ROWS, t), BF16),
                        pltpu.VMEM((t, t), F32),
                        pltpu.VMEM((t, t), F32)]
                       + 2 * _softmax_chain_scratch(t),
        compiler_params=_params(2),
        name="moba_prompt_attention",
    )(qkv, qkv, qkv, km, bias_by_dist)


def _page_specs(n_pages, d_rows):
    def spec(g):
        return pl.BlockSpec((1, d_rows, HEAD_DIM),
                            lambda bi, st, pt: (pt[bi * n_pages + st * PAGES_PER_STEP + g], 0, 0))
    return [spec(g) for g in range(PAGES_PER_STEP)]


def _own_head_columns(rows, rows_per_head):
    shape = (rows, PAGE_SIZE * N_HEADS)
    col_head = lax.broadcasted_iota(jnp.int32, shape, 1) % N_HEADS
    return col_head == lax.broadcasted_iota(jnp.int32, shape, 0) // rows_per_head


def _diff_decode_kernel(pt_ref, qb_ref, q2_ref, kn_ref, vn_ref, *rest, lam_init):
    kt_refs = rest[:PAGES_PER_STEP]
    v_refs = rest[PAGES_PER_STEP:2 * PAGES_PER_STEP]
    bp_ref, b0_ref, ex_ref, lp_ref, sw_ref, o_ref, m_ref, l_ref, acc_ref = rest[2 * PAGES_PER_STEP:]
    st = pl.program_id(1)
    rows = 2 * N_HEADS
    scale = DIFF_DK ** -0.5

    @pl.when(st == 0)
    def _():
        m_ref[...] = jnp.sum(q2_ref[0] * scale * kn_ref[0], axis=1, keepdims=True) + b0_ref[...]
        l_ref[...] = jnp.ones((rows, 1), F32)
        acc_ref[...] = vn_ref[0]

    qb = qb_ref[0]
    s_pages = []
    for g in range(PAGES_PER_STEP):
        prod = kt_refs[g][0] * qb
        s_pages.append(jnp.sum(prod.reshape(rows, DIFF_DK, PAGE_SIZE), axis=1) * scale + bp_ref[g])
    s = jnp.concatenate(s_pages, axis=1)
    m_prev = m_ref[...]
    m = jnp.maximum(m_prev, jnp.max(s, axis=1, keepdims=True))
    pr = jnp.exp(s - m)
    alpha = jnp.exp(m_prev - m)
    m_ref[...] = m
    l_ref[...] = alpha * l_ref[...] + jnp.sum(pr, axis=1, keepdims=True)
    prb = pr.astype(BF16)
    own = _own_head_columns(rows, 2)
    pv = None
    for g in range(PAGES_PER_STEP):
        spread = jnp.dot(prb[:, g * PAGE_SIZE:(g + 1) * PAGE_SIZE], ex_ref[...], preferred_element_type=F32)
        part = jnp.dot(jnp.where(own, spread, 0.0).astype(BF16), v_refs[g][0].astype(BF16),
                       preferred_element_type=F32)
        pv = part if pv is None else pv + part
    acc_ref[...] = alpha * acc_ref[...] + pv

    @pl.when(st == pl.num_programs(1) - 1)
    def _():
        acc_ref[...] = acc_ref[...] * (1.0 / l_ref[...])
        lam = _diff_lambda(lp_ref[...], lam_init)
        od = acc_ref[pl.ds(0, N_HEADS, stride=2), :] - lam * acc_ref[pl.ds(1, N_HEADS, stride=2), :]
        od = od * lax.rsqrt(jnp.mean(od * od, axis=1, keepdims=True) + SUBLN_EPS)
        o_ref[0] = (od * sw_ref[...] * (1.0 - lam_init)).astype(BF16)


def _diff_decode_attention(qkv, cache_kt, cache_v, page_table, bias_pages, bias_self,
                           lam_params, subln_w, lam_init):
    _, b, d = qkv.shape
    n_pages = page_table.shape[1]
    assert n_pages % PAGES_PER_STEP == 0
    rows = 2 * N_HEADS
    q, kn, vn = qkv[0], qkv[1], qkv[2]
    qb = jnp.broadcast_to(q[:, :, None], (b, d, PAGE_SIZE))
    q2 = q.reshape(b, rows, DIFF_DK)
    kn2 = kn.reshape(b, rows, DIFF_DK)
    vn2 = jnp.repeat(vn.reshape(b, N_HEADS, HEAD_DIM), 2, axis=1)
    bp = jnp.repeat(bias_pages, 2, axis=1)
    b0 = jnp.repeat(bias_self, 2, axis=0)
    spread = (jnp.arange(PAGE_SIZE)[:, None] == jnp.arange(PAGE_SIZE * N_HEADS)[None, :] // N_HEADS).astype(BF16)

    per_seq = lambda bi, st, pt: (bi, 0, 0)
    const2 = lambda bi, st, pt: (0, 0)
    out = pl.pallas_call(
        functools.partial(_diff_decode_kernel, lam_init=lam_init),
        grid_spec=pltpu.PrefetchScalarGridSpec(
            num_scalar_prefetch=1,
            grid=(b, n_pages // PAGES_PER_STEP),
            in_specs=[pl.BlockSpec((1, d, PAGE_SIZE), per_seq),
                      pl.BlockSpec((1, rows, DIFF_DK), per_seq),
                      pl.BlockSpec((1, rows, DIFF_DK), per_seq),
                      pl.BlockSpec((1, rows, HEAD_DIM), per_seq)]
                     + _page_specs(n_pages, d) + _page_specs(n_pages, d)
                     + [pl.BlockSpec((PAGES_PER_STEP, rows, PAGE_SIZE), lambda bi, st, pt: (st, 0, 0)),
                        pl.BlockSpec((rows, 1), const2),
                        pl.BlockSpec((PAGE_SIZE, PAGE_SIZE * N_HEADS), const2),
                        pl.BlockSpec((4, DIFF_DK), const2),
                        pl.BlockSpec((1, HEAD_DIM), const2)],
            out_specs=pl.BlockSpec((1, N_HEADS, HEAD_DIM), per_seq),
            scratch_shapes=[pltpu.VMEM((rows, 1), F32),
                            pltpu.VMEM((rows, 1), F32),
                            pltpu.VMEM((rows, HEAD_DIM), F32)]),
        out_shape=jax.ShapeDtypeStruct((b, N_HEADS, HEAD_DIM), BF16),
        compiler_params=_params(2),
        name="diff_decode_attention",
    )(page_table.reshape(-1), qb, q2, kn2, vn2,
      *([cache_kt] * PAGES_PER_STEP), *([cache_v] * PAGES_PER_STEP),
      bp, b0, spread, lam_params, subln_w.reshape(1, HEAD_DIM))
    return out.reshape(b, d)


def _moba_decode_kernel(pt_ref, q_ref, kn_ref, vn_ref, *rest):
    k_refs = rest[:PAGES_PER_STEP]
    v_refs = rest[PAGES_PER_STEP:2 * PAGES_PER_STEP]
    bp_ref, b0_ref, o_ref, m_ref, l_ref, acc_ref, ks_ref = rest[2 * PAGES_PER_STEP:]
    st = pl.program_id(1)
    n_blocks = ks_ref.shape[0]
    scale = HEAD_DIM ** -0.5
    cols = PAGE_SIZE * N_HEADS
    own = _own_head_columns(N_HEADS, 1)

    q = q_ref[0]
    qs = (q * scale).astype(BF16)
    for bb in range(BLOCKS_PER_STEP):
        pages = range(bb * PAGES_PER_BLOCK, (bb + 1) * PAGES_PER_BLOCK)
        s_pages = []
        ksum = jnp.zeros((N_HEADS, HEAD_DIM), F32)
        for g in pages:
            k = k_refs[g][0]
            ksum = ksum + jnp.sum(k.reshape(PAGE_SIZE, N_HEADS, HEAD_DIM), axis=0)
            sg = lax.dot_general(qs, k.astype(BF16), _NT, preferred_element_type=F32) + bp_ref[g]
            s_pages.append(jnp.where(own, sg, NEG))
        s = jnp.concatenate(s_pages, axis=1)
        m = jnp.max(s, axis=1, keepdims=True)
        pr = jnp.exp(s - m)
        prb = pr.astype(BF16)
        pv = None
        for n, g in enumerate(pages):
            part = jnp.dot(prb[:, n * cols:(n + 1) * cols], v_refs[g][0].astype(BF16),
                           preferred_element_type=F32)
            pv = part if pv is None else pv + part
        blk = st * BLOCKS_PER_STEP + bb
        m_ref[blk] = m
        l_ref[blk] = jnp.sum(pr, axis=1, keepdims=True)
        acc_ref[blk] = pv
        ks_ref[blk] = ksum

    @pl.when(st == pl.num_programs(1) - 1)
    def _():
        gates = [jnp.sum(q * (ks_ref[n] * (1.0 / MOBA_BLOCK)), axis=1, keepdims=True)
                 for n in range(n_blocks)]
        s_self = jnp.sum(q * scale * kn_ref[0], axis=1, keepdims=True) + b0_ref[...]
        m_tot = s_self
        sels = []
        for n in range(n_blocks):
            rank = jnp.zeros((N_HEADS, 1), F32)
            for n2 in range(n_blocks):
                if n2 < n:
                    rank = rank + jnp.where(gates[n2] >= gates[n], 1.0, 0.0)
                elif n2 > n:
                    rank = rank + jnp.where(gates[n2] > gates[n], 1.0, 0.0)
            sel = rank < float(min(MOBA_TOPK, n_blocks))
            sels.append(sel)
            m_tot = jnp.maximum(m_tot, jnp.where(sel, m_ref[n], NEG))
        w_self = jnp.exp(s_self - m_tot)
        l_tot = w_self
        o = w_self * vn_ref[0]
        for n in range(n_blocks):
            w = jnp.where(sels[n], jnp.exp(m_ref[n] - m_tot), 0.0)
            l_tot = l_tot + w * l_ref[n]
            o = o + w * acc_ref[n]
        o_ref[0] = (o * (1.0 / l_tot)).astype(BF16)


def _moba_decode_attention(qkv, cache_k, cache_v, page_table, bias_pages, bias_self):
    _, b, d = qkv.shape
    n_pages = page_table.shape[1]
    assert n_pages % PAGES_PER_STEP == 0
    n_blocks = n_pages // PAGES_PER_BLOCK
    q3 = qkv.reshape(3, b, N_HEADS, HEAD_DIM)

    def qkv_spec(which):
        return pl.BlockSpec((None, 1, N_HEADS, HEAD_DIM), lambda bi, st, pt: (which, bi, 0, 0))

    out = pl.pallas_call(
        _moba_decode_kernel,
        grid_spec=pltpu.PrefetchScalarGridSpec(
            num_scalar_prefetch=1,
            grid=(b, n_pages // PAGES_PER_STEP),
            in_specs=[qkv_spec(0), qkv_spec(1), qkv_spec(2)]
                     + _page_specs(n_pages, d) + _page_specs(n_pages, d)
                     + [pl.BlockSpec((PAGES_PER_STEP, N_HEADS, PAGE_SIZE * N_HEADS),
                                     lambda bi, st, pt: (st, 0, 0)),
                        pl.BlockSpec((N_HEADS, 1), lambda bi, st, pt: (0, 0))],
            out_specs=pl.BlockSpec((1, N_HEADS, HEAD_DIM), lambda bi, st, pt: (bi, 0, 0)),
            scratch_shapes=[pltpu.VMEM((n_blocks, N_HEADS, 1), F32),
                            pltpu.VMEM((n_blocks, N_HEADS, 1), F32),
                            pltpu.VMEM((n_blocks, N_HEADS, HEAD_DIM), F32),
                            pltpu.VMEM((n_blocks, N_HEADS, HEAD_DIM), F32)]),
        out_shape=jax.ShapeDtypeStruct((b, N_HEADS, HEAD_DIM), BF16),
        compiler_params=_params(2),
        name="moba_decode_attention",
    )(page_table.reshape(-1), q3, q3, q3,
      *([cache_k] * PAGES_PER_STEP), *([cache_v] * PAGES_PER_STEP),
      jnp.repeat(bias_pages, N_HEADS, axis=2), bias_self)
    return out.reshape(b, d)


def _pages_pos_head(cache):
    n_phys = cache.shape[0]
    return cache.reshape(n_phys, PAGE_SIZE * N_HEADS, HEAD_DIM)


def _pages_key_major(cache):
    n_phys = cache.shape[0]
    return jnp.transpose(cache, (0, 2, 3, 4, 1)).reshape(n_phys, N_HEADS * HEAD_DIM, PAGE_SIZE)


def _row_block(m):
    return 512 if m % 512 == 0 else m


def kernel(x_prompt, x_sample, cache_k_diff, cache_v_diff, cache_k_moba, cache_v_moba, state_conv, page_table, rel_bias, norm_mix, norm_ffn, norm_final, w_qkv_diff, lambda_diff, subln_diff, w_o_diff, w_qkv_moba, w_o_moba, w_up, conv_w, conv_b, w_down):
    bp_, s, d = x_prompt.shape
    bs_, ts, _ = x_sample.shape
    assert bp_ == 1 and ts == 1 and d == N_HEADS * HEAD_DIM
    depth = norm_mix.shape[0]
    past_len = page_table.shape[1] * PAGE_SIZE
    dff = conv_w.shape[-1]
    bn_ff = 512 if dff % 512 == 0 else dff

    hp = x_prompt.reshape(s, d)
    hs = x_sample.reshape(bs_, d)
    bm_p, bm_s = _row_block(s), _row_block(bs_)

    bias_by_dist = _prompt_bias_by_distance(rel_bias, ATT_TILE)
    bias_pages, bias_self = _decode_bias_tables(rel_bias, past_len)

    kd_p, vd_p, km_p, vm_p, cv_p = [], [], [], [], []
    kd_s, vd_s, km_s, vm_s, cv_s = [], [], [], [], []
    for i in range(depth):
        slot = i // 2
        if i % 2 == 0:
            li = _lambda_init(i)
            w_qkv = w_qkv_diff[slot].astype(BF16)
            w_o = w_o_diff[slot].astype(BF16)
            qkv_p = _norm_qkv(hp, norm_mix[i], w_qkv, bm=bm_p, bn=1024)
            qkv_s = _norm_qkv(hs, norm_mix[i], w_qkv, bm=bm_s, bn=1024)
            o_p = _diff_prompt_attention(qkv_p, bias_by_dist, lambda_diff[slot], subln_diff[slot], li)
            o_s = _diff_decode_attention(qkv_s, _pages_key_major(cache_k_diff[slot]),
                                         _pages_pos_head(cache_v_diff[slot]), page_table,
                                         bias_pages, bias_self, lambda_diff[slot], subln_diff[slot], li)
            kd_p.append(qkv_p[1].reshape(1, s, N_HEADS, 2, DIFF_DK))
            vd_p.append(qkv_p[2].reshape(1, s, N_HEADS, HEAD_DIM))
            kd_s.append(qkv_s[1].reshape(bs_, 1, N_HEADS, 2, DIFF_DK))
            vd_s.append(qkv_s[2].reshape(bs_, 1, N_HEADS, HEAD_DIM))
        else:
            w_qkv = w_qkv_moba[slot].astype(BF16)
            w_o = w_o_moba[slot].astype(BF16)
            qkv_p = _norm_qkv(hp, norm_mix[i], w_qkv, bm=bm_p, bn=1024)
            qkv_s = _norm_qkv(hs, norm_mix[i], w_qkv, bm=bm_s, bn=1024)
            o_p = _moba_prompt_attention(qkv_p, _block_means(qkv_p), bias_by_dist)
            o_s = _moba_decode_attention(qkv_s, _pages_pos_head(cache_k_moba[slot]),
                                         _pages_pos_head(cache_v_moba[slot]), page_table,
                                         bias_pages, bias_self)
            km_p.append(qkv_p[1].reshape(1, s, N_HEADS, HEAD_DIM))
            vm_p.append(qkv_p[2].reshape(1, s, N_HEADS, HEAD_DIM))
            km_s.append(qkv_s[1].reshape(bs_, 1, N_HEADS, HEAD_DIM))
            vm_s.append(qkv_s[2].reshape(bs_, 1, N_HEADS, HEAD_DIM))
        hp = _matmul_residual(o_p, w_o, hp, bm=bm_p, bn=512)
        hs = _matmul_residual(o_s, w_o, hs, bm=bm_s, bn=512)

        wu = w_up[i].astype(BF16)
        wd = w_down[i].astype(BF16)
        f_p, st_p = _ffn_up_seq(hp, norm_ffn[i], wu, conv_w[i], conv_b[i], bm=bm_p, bn=bn_ff)
        f_s, st_s = _ffn_up_tok(hs, norm_ffn[i], wu, conv_w[i], conv_b[i], state_conv[i], bn=bn_ff)
        hp = _matmul_residual(f_p, wd, hp, bm=bm_p, bn=512)
        hs = _matmul_residual(f_s, wd, hs, bm=bm_s, bn=512)
        cv_p.append(st_p.reshape(1, CONV_W - 1, dff))
        cv_s.append(st_s)

    y_p = _rmsnorm(hp, norm_final, bm=bm_p).reshape(1, s, d)
    y_s = _rmsnorm(hs, norm_final, bm=bm_s).reshape(bs_, 1, d)
    return (y_p, y_s,
            jnp.stack(kd_p), jnp.stack(vd_p), jnp.stack(km_p), jnp.stack(vm_p), jnp.stack(cv_p),
            jnp.stack(kd_s), jnp.stack(vd_s), jnp.stack(km_s), jnp.stack(vm_s), jnp.stack(cv_s))
```

```python
import functools
import math

import jax
import jax.numpy as jnp
from jax import lax
from jax.experimental import pallas as pl
from jax.experimental.pallas import tpu as pltpu

F32 = jnp.float32
BF16 = jnp.bfloat16

N_HEADS = 16
HEAD_DIM = 128
DIFF_DK = HEAD_DIM // 2
MOBA_BLOCK = 256
MOBA_TOPK = 3
PAGE_SIZE = 128
CONV_W = 3
N_BUCKETS = 32
MAX_EXACT = N_BUCKETS // 2
MAX_DISTANCE = 128
EPS = 1e-6
SUBLN_EPS = 1e-5
NEG = -1e30
LOG2E = 1.4426950408889634
ATT_TILE = 512
PAGES_PER_BLOCK = MOBA_BLOCK // PAGE_SIZE
BLOCKS_PER_STEP = 2
PAGES_PER_STEP = PAGES_PER_BLOCK * BLOCKS_PER_STEP
VMEM_LIMIT = 56 * 1024 * 1024

_NT = (((1,), (1,)), ((), ()))


def _lambda_init(i):
    return 0.8 - 0.6 * math.exp(-0.3 * i)


def _params(n_grid_dims):
    return pltpu.CompilerParams(dimension_semantics=("arbitrary",) * n_grid_dims,
                                vmem_limit_bytes=VMEM_LIMIT)


def _norm_rows(x, g):
    r = lax.rsqrt(jnp.mean(x * x, axis=-1, keepdims=True) + EPS)
    return x * r * g


def _norm_matmul_kernel(x_ref, g_ref, w_ref, o_ref, xn_ref):
    @pl.when(pl.program_id(1) == 0)
    def _():
        xn_ref[...] = _norm_rows(x_ref[...], g_ref[...]).astype(BF16)

    o_ref[0] = jnp.dot(xn_ref[...], w_ref[...], preferred_element_type=F32)


def _norm_proj(x, g, w, which, *, bm, bn):
    m, d = x.shape
    nb = d // bn
    return pl.pallas_call(
        _norm_matmul_kernel,
        grid=(m // bm, nb),
        in_specs=[pl.BlockSpec((bm, d), lambda i, j: (i, 0)),
                  pl.BlockSpec((1, d), lambda i, j: (0, 0)),
                  pl.BlockSpec((d, bn), lambda i, j: (0, which * nb + j))],
        out_specs=pl.BlockSpec((1, bm, bn), lambda i, j: (0, i, j)),
        out_shape=jax.ShapeDtypeStruct((1, m, d), F32),
        scratch_shapes=[pltpu.VMEM((bm, d), BF16)],
        compiler_params=_params(2),
        name="norm_proj",
    )(x, g.reshape(1, d), w)


def _mm_res_kernel(a_ref, w_ref, r_ref, o_ref):
    o_ref[...] = r_ref[...] + jnp.dot(a_ref[...], w_ref[...], preferred_element_type=F32)


def _matmul_residual(a, w, res, *, bm, bn):
    m, k = a.shape
    n = w.shape[1]
    return pl.pallas_call(
        _mm_res_kernel,
        grid=(m // bm, n // bn),
        in_specs=[pl.BlockSpec((bm, k), lambda i, j: (i, 0)),
                  pl.BlockSpec((k, bn), lambda i, j: (0, j)),
                  pl.BlockSpec((bm, bn), lambda i, j: (i, j))],
        out_specs=pl.BlockSpec((bm, bn), lambda i, j: (i, j)),
        out_shape=jax.ShapeDtypeStruct((m, n), F32),
        compiler_params=_params(2),
        name="matmul_residual",
    )(a, w, res)


def _rmsnorm_kernel(x_ref, g_ref, o_ref):
    o_ref[...] = _norm_rows(x_ref[...], g_ref[...])


def _rmsnorm(x, g, *, bm):
    m, d = x.shape
    return pl.pallas_call(
        _rmsnorm_kernel,
        grid=(m // bm,),
        in_specs=[pl.BlockSpec((bm, d), lambda i: (i, 0)),
                  pl.BlockSpec((1, d), lambda i: (0, 0))],
        out_specs=pl.BlockSpec((bm, d), lambda i: (i, 0)),
        out_shape=jax.ShapeDtypeStruct((m, d), F32),
        compiler_params=_params(1),
        name="final_rmsnorm",
    )(x, g.reshape(1, d))


FFN_CHUNK = 256


def _silu_gate(gc, u):
    return gc * (1.0 / (1.0 + jnp.exp(-gc))) * u


def _ffn_up_seq_kernel(x_ref, g_ref, wg_ref, wu_ref, cw_ref, cb_ref, h_ref, st_ref,
                       xn_ref, gbuf_ref, carry_ref, *, bm):
    i = pl.program_id(0)
    j = pl.program_id(1)

    @pl.when(j == 0)
    def _():
        xn_ref[...] = _norm_rows(x_ref[...], g_ref[...]).astype(BF16)

    bn = h_ref.shape[1]

    @pl.when(i == 0)
    def _():
        gbuf_ref[0:8, :] = jnp.zeros((8, bn), F32)

    @pl.when(i > 0)
    def _():
        gbuf_ref[0:8, :] = carry_ref[j]

    xn = xn_ref[...]
    cw = cw_ref[...]
    cb = cb_ref[...]
    for c0 in range(0, bn, FFN_CHUNK):
        cs = slice(c0, c0 + FFN_CHUNK)
        g = jnp.dot(xn, wg_ref[:, cs], preferred_element_type=F32)
        u = jnp.dot(xn, wu_ref[:, cs], preferred_element_type=F32)
        gbuf_ref[8:8 + bm, cs] = g
        g1 = gbuf_ref[7:7 + bm, cs]
        g2 = gbuf_ref[6:6 + bm, cs]
        gc = cb[:, cs] + cw[2:3, cs] * g
        gc = gc + cw[0:1, cs] * g2
        gc = gc + cw[1:2, cs] * g1
        h_ref[:, cs] = _silu_gate(gc, u).astype(BF16)
    last8 = gbuf_ref[bm:bm + 8, :]
    carry_ref[j] = last8
    st_ref[0] = last8


def _ffn_up_seq(x, g, w_up, conv_w, conv_b, *, bm, bn):
    m, d = x.shape
    dff = conv_w.shape[1]
    nj = dff // bn
    h, st = pl.pallas_call(
        functools.partial(_ffn_up_seq_kernel, bm=bm),
        grid=(m // bm, nj),
        in_specs=[pl.BlockSpec((bm, d), lambda i, j: (i, 0)),
                  pl.BlockSpec((1, d), lambda i, j: (0, 0)),
                  pl.BlockSpec((d, bn), lambda i, j: (0, j)),
                  pl.BlockSpec((d, bn), lambda i, j: (0, j + nj)),
                  pl.BlockSpec((CONV_W, bn), lambda i, j: (0, j)),
                  pl.BlockSpec((1, bn), lambda i, j: (0, j))],
        out_specs=[pl.BlockSpec((bm, bn), lambda i, j: (i, j)),
                   pl.BlockSpec((1, 8, bn), lambda i, j: (i, 0, j))],
        out_shape=[jax.ShapeDtypeStruct((m, dff), BF16),
                   jax.ShapeDtypeStruct((m // bm, 8, dff), F32)],
        scratch_shapes=[pltpu.VMEM((bm, d), BF16),
                        pltpu.VMEM((bm + 8, bn), F32),
                        pltpu.VMEM((nj, 8, bn), F32)],
        compiler_params=_params(2),
        name="ffn_up_seq",
    )(x, g.reshape(1, d), w_up, w_up, conv_w, conv_b.reshape(1, dff))
    return h, st[-1, 8 - (CONV_W - 1):]


def _ffn_up_tok_kernel(x_ref, g_ref, wg_ref, wu_ref, cw_ref, cb_ref, p0_ref, p1_ref,
                       h_ref, gout_ref, xn_ref):
    @pl.when(pl.program_id(0) == 0)
    def _():
        xn_ref[...] = _norm_rows(x_ref[...], g_ref[...]).astype(BF16)

    xn = xn_ref[...]
    g = jnp.dot(xn, wg_ref[...], preferred_element_type=F32)
    u = jnp.dot(xn, wu_ref[...], preferred_element_type=F32)
    cw = cw_ref[...]
    gc = cb_ref[...] + cw[2:3] * g
    gc = gc + cw[0:1] * p0_ref[...]
    gc = gc + cw[1:2] * p1_ref[...]
    h_ref[...] = _silu_gate(gc, u).astype(BF16)
    gout_ref[...] = g


def _ffn_up_tok(x, g, w_up, conv_w, conv_b, prev, *, bn):
    m, d = x.shape
    dff = conv_w.shape[1]
    nj = dff // bn
    h, gout = pl.pallas_call(
        _ffn_up_tok_kernel,
        grid=(nj,),
        in_specs=[pl.BlockSpec((m, d), lambda j: (0, 0)),
                  pl.BlockSpec((1, d), lambda j: (0, 0)),
                  pl.BlockSpec((d, bn), lambda j: (0, j)),
                  pl.BlockSpec((d, bn), lambda j: (0, j + nj)),
                  pl.BlockSpec((CONV_W, bn), lambda j: (0, j)),
                  pl.BlockSpec((1, bn), lambda j: (0, j)),
                  pl.BlockSpec((m, bn), lambda j: (0, j)),
                  pl.BlockSpec((m, bn), lambda j: (0, j))],
        out_specs=[pl.BlockSpec((m, bn), lambda j: (0, j)),
                   pl.BlockSpec((m, bn), lambda j: (0, j))],
        out_shape=[jax.ShapeDtypeStruct((m, dff), BF16),
                   jax.ShapeDtypeStruct((m, dff), F32)],
        scratch_shapes=[pltpu.VMEM((m, d), BF16)],
        compiler_params=_params(1),
        name="ffn_up_tok",
    )(x, g.reshape(1, d), w_up, w_up, conv_w, conv_b.reshape(1, dff), prev[:, 0], prev[:, 1])
    return h, jnp.stack([prev[:, 1], gout], axis=1)


def _rel_bucket(dist):
    n = jnp.maximum(dist, 0)
    nf = jnp.maximum(n, MAX_EXACT).astype(F32)
    large = MAX_EXACT + (jnp.log(nf / MAX_EXACT) / math.log(MAX_DISTANCE / MAX_EXACT)
                         * (N_BUCKETS - MAX_EXACT)).astype(jnp.int32)
    large = jnp.minimum(large, N_BUCKETS - 1)
    return jnp.where(n < MAX_EXACT, n, large)


def _prompt_bias_by_distance(rel_bias, t):
    assert t >= MAX_DISTANCE
    far = rel_bias[N_BUCKETS - 1]
    by_dist = rel_bias[_rel_bucket(jnp.arange(2 * t))] - far[None, :]
    return by_dist.T.reshape(N_HEADS, 1, 2 * t).astype(F32)


def _decode_bias_tables(rel_bias, past_len):
    far = rel_bias[N_BUCKETS - 1]
    dist = past_len - jnp.arange(past_len)
    pages = (rel_bias[_rel_bucket(dist)] - far[None, :]).reshape(past_len // PAGE_SIZE, PAGE_SIZE, N_HEADS)
    self_bias = (rel_bias[0] - far).reshape(N_HEADS, 1)
    return jnp.transpose(pages, (0, 2, 1)).astype(F32), self_bias.astype(F32)


def _fill_bias_tiles(bb_ref, bd_ref, bs_ref, t):
    w = bb_ref[0] * LOG2E
    lane = lax.broadcasted_iota(jnp.int32, (1, 2 * t), 1)
    w_causal = jnp.where(lane >= t, pltpu.roll(w, t, 1), NEG)
    rolled = pltpu.roll(jnp.broadcast_to(w_causal, (t, 2 * t)), 0, 1, stride=1, stride_axis=0)
    bd_ref[...] = rolled[:, t:]
    rolled = pltpu.roll(jnp.broadcast_to(w, (t, 2 * t)), 0, 1, stride=1, stride_axis=0)
    bs_ref[...] = rolled[:, t:]


def _diff_lambda(lp, lam_init):
    a = jnp.sum(lp[0:1] * lp[1:2], axis=-1, keepdims=True)
    b = jnp.sum(lp[2:3] * lp[3:4], axis=-1, keepdims=True)
    return jnp.exp(a) - jnp.exp(b) + lam_init


ONES_ROWS = 16


def _values_t_with_ones(v):
    return jnp.concatenate([v.T, jnp.ones((ONES_ROWS, v.shape[0]), F32)], axis=0).astype(BF16)


def _flash_first(s, vt, m_ref, acc_ref):
    m = jnp.max(s, axis=0, keepdims=True)
    p = jnp.exp2(s - m)
    m_ref[...] = m
    acc_ref[...] = jnp.dot(vt, p.astype(BF16), preferred_element_type=F32)


def _flash_next(s, vt, m_ref, acc_ref):
    m_prev = m_ref[...]
    m = jnp.maximum(m_prev, jnp.max(s, axis=0, keepdims=True))
    p = jnp.exp2(s - m)
    alpha = jnp.exp2(m_prev - m)
    m_ref[...] = m
    acc_ref[...] = alpha * acc_ref[...] + jnp.dot(vt, p.astype(BF16), preferred_element_type=F32)


def _normalized(acc_ref):
    return acc_ref[0:HEAD_DIM, :] * (1.0 / acc_ref[HEAD_DIM:HEAD_DIM + 1, :])


def _causal_key_tiles(i, scores, vt_ref, bias_diag, bias_sub, chain_a, chain_b):
    @pl.when(i == 0)
    def _():
        _flash_first(scores(i) + bias_diag(), vt_ref[i], *chain_a)

    @pl.when(i > 0)
    def _():
        s_a = scores(i) + bias_diag()
        s_b = scores(i - 1) + bias_sub()
        _flash_first(s_a, vt_ref[i], *chain_a)
        _flash_first(s_b, vt_ref[i - 1], *chain_b)

    def softmax_step(s, m_ref):
        m_prev = m_ref[...]
        m = jnp.maximum(m_prev, jnp.max(s, axis=0, keepdims=True))
        p = jnp.exp2(s - m)
        alpha = jnp.exp2(m_prev - m)
        m_ref[...] = m
        return p.astype(BF16), alpha

    def pair(jj, carry):
        s_a = scores(2 * jj)
        s_b = scores(2 * jj + 1)
        p_a, alpha_a = softmax_step(s_a, chain_a[0])
        pv_a = jnp.dot(vt_ref[2 * jj], p_a, preferred_element_type=F32)
        p_b, alpha_b = softmax_step(s_b, chain_b[0])
        pv_b = jnp.dot(vt_ref[2 * jj + 1], p_b, preferred_element_type=F32)
        chain_a[1][...] = alpha_a * chain_a[1][...] + pv_a
        chain_b[1][...] = alpha_b * chain_b[1][...] + pv_b
        return carry
    lax.fori_loop(0, (i - 1) // 2, pair, 0)

    @pl.when(jnp.logical_and(i >= 2, i % 2 == 0))
    def _():
        _flash_next(scores(i - 2), vt_ref[i - 2], *chain_a)

    @pl.when(i > 0)
    def _():
        (ma_ref, acca_ref), (mb_ref, accb_ref) = chain_a, chain_b
        ma, mb = ma_ref[...], mb_ref[...]
        m = jnp.maximum(ma, mb)
        wa, wb = jnp.exp2(ma - m), jnp.exp2(mb - m)
        ma_ref[...] = m
        acca_ref[...] = wa * acca_ref[...] + wb * accb_ref[...]


def _softmax_chain_scratch(width):
    return [pltpu.VMEM((1, width), F32), pltpu.VMEM((HEAD_DIM + ONES_ROWS, width), F32)]


def _diff_prompt_kernel(q_ref, k_ref, v_ref, bb_ref, lp_ref, sw_ref, o_ref,
                        kb_ref, vt_ref, bd_ref, bs_ref, m_ref, acc_ref, m2_ref, acc2_ref, *, t, lam_init):
    i = pl.program_id(1)
    n_tiles = kb_ref.shape[0]

    @pl.when(i == 0)
    def _():
        _fill_bias_tiles(bb_ref, bd_ref, bs_ref, t)

        def fill(c, carry):
            r0 = pl.multiple_of(c * t, t)
            kb_ref[c] = k_ref[0, pl.ds(r0, t), :].astype(BF16)
            vt_ref[c] = _values_t_with_ones(v_ref[0, pl.ds(r0, t), :])
            return carry
        lax.fori_loop(0, n_tiles, fill, 0)

    q = q_ref[0] * (DIFF_DK ** -0.5 * LOG2E)
    lane = lax.broadcasted_iota(jnp.int32, q.shape, 1)
    q0 = jnp.where(lane < DIFF_DK, q, 0.0)
    q1 = jnp.where(lane >= DIFF_DK, q, 0.0)
    qt = jnp.concatenate([q0.T, q1.T], axis=1).astype(BF16)

    def scores(j):
        return jnp.dot(kb_ref[j], qt, preferred_element_type=F32)

    def both_halves(b_ref):
        return lambda: jnp.concatenate([b_ref[...], b_ref[...]], axis=1)

    _causal_key_tiles(i, scores, vt_ref, both_halves(bd_ref), both_halves(bs_ref),
                      (m_ref, acc_ref), (m2_ref, acc2_ref))

    o2 = _normalized(acc_ref)
    lam = _diff_lambda(lp_ref[...], lam_init)
    o = o2[:, :t] - lam * o2[:, t:]
    o = o * lax.rsqrt(jnp.mean(o * o, axis=0, keepdims=True) + SUBLN_EPS)
    o_ref[...] = (o.T * sw_ref[...] * (1.0 - lam_init)).astype(BF16)


def _diff_prompt_attention(q, k, v, bias_by_dist, lam_params, subln_w, lam_init):
    _, s, d = q.shape
    t = ATT_TILE
    nt = s // t
    return pl.pallas_call(
        functools.partial(_diff_prompt_kernel, t=t, lam_init=lam_init),
        grid=(N_HEADS, nt),
        in_specs=[pl.BlockSpec((1, t, HEAD_DIM), lambda h, i: (0, i, h)),
                  pl.BlockSpec((1, s, HEAD_DIM), lambda h, i: (0, 0, h)),
                  pl.BlockSpec((1, s, HEAD_DIM), lambda h, i: (0, 0, h)),
                  pl.BlockSpec((1, 1, 2 * t), lambda h, i: (h, 0, 0)),
                  pl.BlockSpec((4, DIFF_DK), lambda h, i: (0, 0)),
                  pl.BlockSpec((1, HEAD_DIM), lambda h, i: (0, 0))],
        out_specs=pl.BlockSpec((t, HEAD_DIM), lambda h, i: (i, h)),
        out_shape=jax.ShapeDtypeStruct((s, d), BF16),
        scratch_shapes=[pltpu.VMEM((nt, t, HEAD_DIM), BF16),
                        pltpu.VMEM((nt, HEAD_DIM + ONES_ROWS, t), BF16),
                        pltpu.VMEM((t, t), F32),
                        pltpu.VMEM((t, t), F32)]
                       + 2 * _softmax_chain_scratch(2 * t),
        compiler_params=_params(2),
        name="diff_prompt_attention",
    )(q, k, v, bias_by_dist, lam_params, subln_w.reshape(1, HEAD_DIM))


def _block_mean_kernel(k_ref, o_ref):
    o_ref[0] = jnp.sum(k_ref[0], axis=0, keepdims=True) * (1.0 / MOBA_BLOCK)


def _block_means(k):
    _, s, d = k.shape
    nb = s // MOBA_BLOCK
    out = pl.pallas_call(
        _block_mean_kernel,
        grid=(nb,),
        in_specs=[pl.BlockSpec((1, MOBA_BLOCK, d), lambda n: (0, n, 0))],
        out_specs=pl.BlockSpec((1, 1, d), lambda n: (n, 0, 0)),
        out_shape=jax.ShapeDtypeStruct((nb, 1, d), F32),
        compiler_params=_params(1),
        name="moba_block_means",
    )(k)
    return out.reshape(nb, d)


def _split_bf16(x):
    hi = x.astype(BF16)
    lo = (x - hi.astype(F32)).astype(BF16)
    return hi, lo


def _moba_prompt_kernel(q_ref, k_ref, v_ref, km_ref, bb_ref, o_ref,
                        ka_ref, vt_ref, bd_ref, bs_ref, m_ref, acc_ref, m2_ref, acc2_ref, *, t):
    i = pl.program_id(1)
    n_tiles = ka_ref.shape[0]
    nb = km_ref.shape[0]
    blocks_per_tile = t // MOBA_BLOCK

    @pl.when(i == 0)
    def _():
        _fill_bias_tiles(bb_ref, bd_ref, bs_ref, t)
        lane = lax.broadcasted_iota(jnp.int32, (t, HEAD_DIM), 1)
        sub = lax.broadcasted_iota(jnp.int32, (t, HEAD_DIM), 0)

        def fill(c, carry):
            r0 = pl.multiple_of(c * t, t)
            ka_ref[c, :, 0:HEAD_DIM] = k_ref[0, pl.ds(r0, t), :].astype(BF16)
            ka_ref[c, :, HEAD_DIM:] = jnp.where(
                lane == c * blocks_per_tile + sub // MOBA_BLOCK, 1.0, 0.0).astype(BF16)
            vt_ref[c] = _values_t_with_ones(v_ref[0, pl.ds(r0, t), :])
            return carry
        lax.fori_loop(0, n_tiles, fill, 0)

    qt = q_ref[0].T

    km_hi, km_lo = _split_bf16(km_ref[...])
    qt_hi, qt_lo = _split_bf16(qt)
    gate = (jnp.dot(km_hi, qt_hi, preferred_element_type=F32)
            + jnp.dot(km_hi, qt_lo, preferred_element_type=F32)
            + jnp.dot(km_lo, qt_hi, preferred_element_type=F32))
    row = lax.broadcasted_iota(jnp.int32, (nb, t), 0)
    own = i * blocks_per_tile + lax.broadcasted_iota(jnp.int32, (nb, t), 1) // MOBA_BLOCK
    rowf = row.astype(F32)
    g = jnp.where(row < own, gate, NEG)
    sel = row == own
    for _ in range(MOBA_TOPK):
        mx = jnp.max(g, axis=0, keepdims=True)
        is_max = (g == mx) & (mx > 0.5 * NEG)
        first = jnp.min(jnp.where(is_max, rowf, float(nb)), axis=0, keepdims=True)
        pick = rowf == first
        sel = sel | pick
        g = jnp.where(pick, NEG, g)
    mask_rows = jnp.where(sel, 0.0, NEG)
    qa = jnp.concatenate([qt * (HEAD_DIM ** -0.5 * LOG2E), mask_rows,
                          jnp.zeros((HEAD_DIM - nb, t), F32)], axis=0).astype(BF16)

    def scores(j):
        return jnp.dot(ka_ref[j], qa, preferred_element_type=F32)

    _causal_key_tiles(i, scores, vt_ref, lambda: bd_ref[...], lambda: bs_ref[...],
                      (m_ref, acc_ref), (m2_ref, acc2_ref))

    o_ref[...] = _normalized(acc_ref).T.astype(BF16)


def _moba_prompt_attention(q, k, v, km, bias_by_dist):
    _, s, d = q.shape
    t = ATT_TILE
    nt = s // t
    nb = km.shape[0]
    assert nb <= HEAD_DIM and t % MOBA_BLOCK == 0
    return pl.pallas_call(
        functools.partial(_moba_prompt_kernel, t=t),
        grid=(N_HEADS, nt),
        in_specs=[pl.BlockSpec((1, t, HEAD_DIM), lambda h, i: (0, i, h)),
                  pl.BlockSpec((1, s, HEAD_DIM), lambda h, i: (0, 0, h)),
                  pl.BlockSpec((1, s, HEAD_DIM), lambda h, i: (0, 0, h)),
                  pl.BlockSpec((nb, HEAD_DIM), lambda h, i: (0, h)),
                  pl.BlockSpec((1, 1, 2 * t), lambda h, i: (h, 0, 0))],
        out_specs=pl.BlockSpec((t, HEAD_DIM), lambda h, i: (i, h)),
        out_shape=jax.ShapeDtypeStruct((s, d), BF16),
        scratch_shapes=[pltpu.VMEM((nt, t, 2 * HEAD_DIM), BF16),
                        pltpu.VMEM((nt, HEAD_DIM + ONES_ROWS, t), BF16),
                        pltpu.VMEM((t, t), F32),
                        pltpu.VMEM((t, t), F32)]
                       + 2 * _softmax_chain_scratch(t),
        compiler_params=_params(2),
        name="moba_prompt_attention",
    )(q, k, v, km, bias_by_dist)


def _page_specs(n_pages, d_rows):
    def spec(g):
        return pl.BlockSpec((1, d_rows, HEAD_DIM),
                            lambda bi, st, pt: (pt[bi * n_pages + st * PAGES_PER_STEP + g], 0, 0))
    return [spec(g) for g in range(PAGES_PER_STEP)]


def _own_head_columns(rows, rows_per_head):
    shape = (rows, PAGE_SIZE * N_HEADS)
    col_head = lax.broadcasted_iota(jnp.int32, shape, 1) % N_HEADS
    return col_head == lax.broadcasted_iota(jnp.int32, shape, 0) // rows_per_head


def _diff_decode_kernel(pt_ref, qb_ref, q2_ref, kn_ref, vn_ref, *rest, lam_init):
    kt_refs = rest[:PAGES_PER_STEP]
    v_refs = rest[PAGES_PER_STEP:2 * PAGES_PER_STEP]
    bp_ref, b0_ref, ex_ref, lp_ref, sw_ref, o_ref, m_ref, l_ref, acc_ref = rest[2 * PAGES_PER_STEP:]
    st = pl.program_id(1)
    rows = 2 * N_HEADS
    scale = DIFF_DK ** -0.5

    @pl.when(st == 0)
    def _():
        m_ref[...] = jnp.sum(q2_ref[0] * scale * kn_ref[0], axis=1, keepdims=True) + b0_ref[...]
        l_ref[...] = jnp.ones((rows, 1), F32)
        acc_ref[...] = vn_ref[0]

    qb = qb_ref[0]
    s_pages = []
    for g in range(PAGES_PER_STEP):
        prod = kt_refs[g][0] * qb
        s_pages.append(jnp.sum(prod.reshape(rows, DIFF_DK, PAGE_SIZE), axis=1) * scale + bp_ref[g])
    s = jnp.concatenate(s_pages, axis=1)
    m_prev = m_ref[...]
    m = jnp.maximum(m_prev, jnp.max(s, axis=1, keepdims=True))
    pr = jnp.exp(s - m)
    alpha = jnp.exp(m_prev - m)
    m_ref[...] = m
    l_ref[...] = alpha * l_ref[...] + jnp.sum(pr, axis=1, keepdims=True)
    prb = pr.astype(BF16)
    own = _own_head_columns(rows, 2)
    pv = None
    for g in range(PAGES_PER_STEP):
        spread = jnp.dot(prb[:, g * PAGE_SIZE:(g + 1) * PAGE_SIZE], ex_ref[...], preferred_element_type=F32)
        part = jnp.dot(jnp.where(own, spread, 0.0).astype(BF16), v_refs[g][0].astype(BF16),
                       preferred_element_type=F32)
        pv = part if pv is None else pv + part
    acc_ref[...] = alpha * acc_ref[...] + pv

    @pl.when(st == pl.num_programs(1) - 1)
    def _():
        acc_ref[...] = acc_ref[...] * (1.0 / l_ref[...])
        lam = _diff_lambda(lp_ref[...], lam_init)
        od = acc_ref[pl.ds(0, N_HEADS, stride=2), :] - lam * acc_ref[pl.ds(1, N_HEADS, stride=2), :]
        od = od * lax.rsqrt(jnp.mean(od * od, axis=1, keepdims=True) + SUBLN_EPS)
        o_ref[0] = (od * sw_ref[...] * (1.0 - lam_init)).astype(BF16)


def _diff_decode_attention(qkv, cache_kt, cache_v, page_table, bias_pages, bias_self,
                           lam_params, subln_w, lam_init):
    _, b, d = qkv.shape
    n_pages = page_table.shape[1]
    assert n_pages % PAGES_PER_STEP == 0
    rows = 2 * N_HEADS
    q, kn, vn = qkv[0], qkv[1], qkv[2]
    qb = jnp.broadcast_to(q[:, :, None], (b, d, PAGE_SIZE))
    q2 = q.reshape(b, rows, DIFF_DK)
    kn2 = kn.reshape(b, rows, DIFF_DK)
    vn2 = jnp.repeat(vn.reshape(b, N_HEADS, HEAD_DIM), 2, axis=1)
    bp = jnp.repeat(bias_pages, 2, axis=1)
    b0 = jnp.repeat(bias_self, 2, axis=0)
    spread = (jnp.arange(PAGE_SIZE)[:, None] == jnp.arange(PAGE_SIZE * N_HEADS)[None, :] // N_HEADS).astype(BF16)

    per_seq = lambda bi, st, pt: (bi, 0, 0)
    const2 = lambda bi, st, pt: (0, 0)
    out = pl.pallas_call(
        functools.partial(_diff_decode_kernel, lam_init=lam_init),
        grid_spec=pltpu.PrefetchScalarGridSpec(
            num_scalar_prefetch=1,
            grid=(b, n_pages // PAGES_PER_STEP),
            in_specs=[pl.BlockSpec((1, d, PAGE_SIZE), per_seq),
                      pl.BlockSpec((1, rows, DIFF_DK), per_seq),
                      pl.BlockSpec((1, rows, DIFF_DK), per_seq),
                      pl.BlockSpec((1, rows, HEAD_DIM), per_seq)]
                     + _page_specs(n_pages, d) + _page_specs(n_pages, d)
                     + [pl.BlockSpec((PAGES_PER_STEP, rows, PAGE_SIZE), lambda bi, st, pt: (st, 0, 0)),
                        pl.BlockSpec((rows, 1), const2),
                        pl.BlockSpec((PAGE_SIZE, PAGE_SIZE * N_HEADS), const2),
                        pl.BlockSpec((4, DIFF_DK), const2),
                        pl.BlockSpec((1, HEAD_DIM), const2)],
            out_specs=pl.BlockSpec((1, N_HEADS, HEAD_DIM), per_seq),
            scratch_shapes=[pltpu.VMEM((rows, 1), F32),
                            pltpu.VMEM((rows, 1), F32),
                            pltpu.VMEM((rows, HEAD_DIM), F32)]),
        out_shape=jax.ShapeDtypeStruct((b, N_HEADS, HEAD_DIM), BF16),
        compiler_params=_params(2),
        name="diff_decode_attention",
    )(page_table.reshape(-1), qb, q2, kn2, vn2,
      *([cache_kt] * PAGES_PER_STEP), *([cache_v] * PAGES_PER_STEP),
      bp, b0, spread, lam_params, subln_w.reshape(1, HEAD_DIM))
    return out.reshape(b, d)


def _moba_decode_kernel(pt_ref, q_ref, kn_ref, vn_ref, *rest):
    k_refs = rest[:PAGES_PER_STEP]
    v_refs = rest[PAGES_PER_STEP:2 * PAGES_PER_STEP]
    bp_ref, b0_ref, o_ref, m_ref, l_ref, acc_ref, ks_ref = rest[2 * PAGES_PER_STEP:]
    st = pl.program_id(1)
    n_blocks = ks_ref.shape[0]
    scale = HEAD_DIM ** -0.5
    cols = PAGE_SIZE * N_HEADS
    own = _own_head_columns(N_HEADS, 1)

    q = q_ref[0]
    qs = (q * scale).astype(BF16)
    for bb in range(BLOCKS_PER_STEP):
        pages = range(bb * PAGES_PER_BLOCK, (bb + 1) * PAGES_PER_BLOCK)
        s_pages = []
        ksum = jnp.zeros((N_HEADS, HEAD_DIM), F32)
        for g in pages:
            k = k_refs[g][0]
            ksum = ksum + jnp.sum(k.reshape(PAGE_SIZE, N_HEADS, HEAD_DIM), axis=0)
            sg = lax.dot_general(qs, k.astype(BF16), _NT, preferred_element_type=F32) + bp_ref[g]
            s_pages.append(jnp.where(own, sg, NEG))
        s = jnp.concatenate(s_pages, axis=1)
        m = jnp.max(s, axis=1, keepdims=True)
        pr = jnp.exp(s - m)
        prb = pr.astype(BF16)
        pv = None
        for n, g in enumerate(pages):
            part = jnp.dot(prb[:, n * cols:(n + 1) * cols], v_refs[g][0].astype(BF16),
                           preferred_element_type=F32)
            pv = part if pv is None else pv + part
        blk = st * BLOCKS_PER_STEP + bb
        m_ref[blk] = m
        l_ref[blk] = jnp.sum(pr, axis=1, keepdims=True)
        acc_ref[blk] = pv
        ks_ref[blk] = ksum

    @pl.when(st == pl.num_programs(1) - 1)
    def _():
        gates = [jnp.sum(q * (ks_ref[n] * (1.0 / MOBA_BLOCK)), axis=1, keepdims=True)
                 for n in range(n_blocks)]
        s_self = jnp.sum(q * scale * kn_ref[0], axis=1, keepdims=True) + b0_ref[...]
        m_tot = s_self
        sels = []
        for n in range(n_blocks):
            rank = jnp.zeros((N_HEADS, 1), F32)
            for n2 in range(n_blocks):
                if n2 < n:
                    rank = rank + jnp.where(gates[n2] >= gates[n], 1.0, 0.0)
                elif n2 > n:
                    rank = rank + jnp.where(gates[n2] > gates[n], 1.0, 0.0)
            sel = rank < float(min(MOBA_TOPK, n_blocks))
            sels.append(sel)
            m_tot = jnp.maximum(m_tot, jnp.where(sel, m_ref[n], NEG))
        w_self = jnp.exp(s_self - m_tot)
        l_tot = w_self
        o = w_self * vn_ref[0]
        for n in range(n_blocks):
            w = jnp.where(sels[n], jnp.exp(m_ref[n] - m_tot), 0.0)
            l_tot = l_tot + w * l_ref[n]
            o = o + w * acc_ref[n]
        o_ref[0] = (o * (1.0 / l_tot)).astype(BF16)


def _moba_decode_attention(qkv, cache_k, cache_v, page_table, bias_pages, bias_self):
    _, b, d = qkv.shape
    n_pages = page_table.shape[1]
    assert n_pages % PAGES_PER_STEP == 0
    n_blocks = n_pages // PAGES_PER_BLOCK
    q3 = qkv.reshape(3, b, N_HEADS, HEAD_DIM)

    def qkv_spec(which):
        return pl.BlockSpec((None, 1, N_HEADS, HEAD_DIM), lambda bi, st, pt: (which, bi, 0, 0))

    out = pl.pallas_call(
        _moba_decode_kernel,
        grid_spec=pltpu.PrefetchScalarGridSpec(
            num_scalar_prefetch=1,
            grid=(b, n_pages // PAGES_PER_STEP),
            in_specs=[qkv_spec(0), qkv_spec(1), qkv_spec(2)]
                     + _page_specs(n_pages, d) + _page_specs(n_pages, d)
                     + [pl.BlockSpec((PAGES_PER_STEP, N_HEADS, PAGE_SIZE * N_HEADS),
                                     lambda bi, st, pt: (st, 0, 0)),
                        pl.BlockSpec((N_HEADS, 1), lambda bi, st, pt: (0, 0))],
            out_specs=pl.BlockSpec((1, N_HEADS, HEAD_DIM), lambda bi, st, pt: (bi, 0, 0)),
            scratch_shapes=[pltpu.VMEM((n_blocks, N_HEADS, 1), F32),
                            pltpu.VMEM((n_blocks, N_HEADS, 1), F32),
                            pltpu.VMEM((n_blocks, N_HEADS, HEAD_DIM), F32),
                            pltpu.VMEM((n_blocks, N_HEADS, HEAD_DIM), F32)]),
        out_shape=jax.ShapeDtypeStruct((b, N_HEADS, HEAD_DIM), BF16),
        compiler_params=_params(2),
        name="moba_decode_attention",
    )(page_table.reshape(-1), q3, q3, q3,
      *([cache_k] * PAGES_PER_STEP), *([cache_v] * PAGES_PER_STEP),
      jnp.repeat(bias_pages, N_HEADS, axis=2), bias_self)
    return out.reshape(b, d)


def _pages_pos_head(cache):
    n_phys = cache.shape[0]
    return cache.reshape(n_phys, PAGE_SIZE * N_HEADS, HEAD_DIM)


def _pages_key_major(cache):
    n_phys = cache.shape[0]
    return jnp.transpose(cache, (0, 2, 3, 4, 1)).reshape(n_phys, N_HEADS * HEAD_DIM, PAGE_SIZE)


def _row_block(m):
    return 512 if m % 512 == 0 else m


def kernel(x_prompt, x_sample, cache_k_diff, cache_v_diff, cache_k_moba, cache_v_moba, state_conv, page_table, rel_bias, norm_mix, norm_ffn, norm_final, w_qkv_diff, lambda_diff, subln_diff, w_o_diff, w_qkv_moba, w_o_moba, w_up, conv_w, conv_b, w_down):
    bp_, s, d = x_prompt.shape
    bs_, ts, _ = x_sample.shape
    assert bp_ == 1 and ts == 1 and d == N_HEADS * HEAD_DIM
    depth = norm_mix.shape[0]
    past_len = page_table.shape[1] * PAGE_SIZE
    dff = conv_w.shape[-1]
    bn_ff = 512 if dff % 512 == 0 else dff

    hp = x_prompt.reshape(s, d)
    hs = x_sample.reshape(bs_, d)
    bm_p, bm_s = _row_block(s), _row_block(bs_)

    bias_by_dist = _prompt_bias_by_distance(rel_bias, ATT_TILE)
    bias_pages, bias_self = _decode_bias_tables(rel_bias, past_len)

    kd_p, vd_p, km_p, vm_p, cv_p = [], [], [], [], []
    kd_s, vd_s, km_s, vm_s, cv_s = [], [], [], [], []
    for i in range(depth):
        slot = i // 2
        if i % 2 == 0:
            li = _lambda_init(i)
            w_qkv = w_qkv_diff[slot].astype(BF16)
            w_o = w_o_diff[slot].astype(BF16)
            q_p, k_p, v_p = [_norm_proj(hp, norm_mix[i], w_qkv, n, bm=bm_p, bn=1024) for n in range(3)]
            qkv_s = jnp.concatenate([_norm_proj(hs, norm_mix[i], w_qkv, n, bm=bm_s, bn=1024)
                                     for n in range(3)], axis=0)
            o_p = _diff_prompt_attention(q_p, k_p, v_p, bias_by_dist, lambda_diff[slot], subln_diff[slot], li)
            o_s = _diff_decode_attention(qkv_s, _pages_key_major(cache_k_diff[slot]),
                                         _pages_pos_head(cache_v_diff[slot]), page_table,
                                         bias_pages, bias_self, lambda_diff[slot], subln_diff[slot], li)
            kd_p.append(k_p.reshape(1, s, N_HEADS, 2, DIFF_DK))
            vd_p.append(v_p.reshape(1, s, N_HEADS, HEAD_DIM))
            kd_s.append(qkv_s[1].reshape(bs_, 1, N_HEADS, 2, DIFF_DK))
            vd_s.append(qkv_s[2].reshape(bs_, 1, N_HEADS, HEAD_DIM))
        else:
            w_qkv = w_qkv_moba[slot].astype(BF16)
            w_o = w_o_moba[slot].astype(BF16)
            q_p, k_p, v_p = [_norm_proj(hp, norm_mix[i], w_qkv, n, bm=bm_p, bn=1024) for n in range(3)]
            qkv_s = jnp.concatenate([_norm_proj(hs, norm_mix[i], w_qkv, n, bm=bm_s, bn=1024)
                                     for n in range(3)], axis=0)
            o_p = _moba_prompt_attention(q_p, k_p, v_p, _block_means(k_p), bias_by_dist)
            o_s = _moba_decode_attention(qkv_s, _pages_pos_head(cache_k_moba[slot]),
                                         _pages_pos_head(cache_v_moba[slot]), page_table,
                                         bias_pages, bias_self)
            km_p.append(k_p.reshape(1, s, N_HEADS, HEAD_DIM))
            vm_p.append(v_p.reshape(1, s, N_HEADS, HEAD_DIM))
            km_s.append(qkv_s[1].reshape(bs_, 1, N_HEADS, HEAD_DIM))
            vm_s.append(qkv_s[2].reshape(bs_, 1, N_HEADS, HEAD_DIM))
        hp = _matmul_residual(o_p, w_o, hp, bm=bm_p, bn=512)
        hs = _matmul_residual(o_s, w_o, hs, bm=bm_s, bn=512)

        wu = w_up[i].astype(BF16)
        wd = w_down[i].astype(BF16)
        f_p, st_p = _ffn_up_seq(hp, norm_ffn[i], wu, conv_w[i], conv_b[i], bm=bm_p, bn=bn_ff)
        f_s, st_s = _ffn_up_tok(hs, norm_ffn[i], wu, conv_w[i], conv_b[i], state_conv[i], bn=bn_ff)
        hp = _matmul_residual(f_p, wd, hp, bm=bm_p, bn=512)
        hs = _matmul_residual(f_s, wd, hs, bm=bm_s, bn=512)
        cv_p.append(st_p.reshape(1, CONV_W - 1, dff))
        cv_s.append(st_s)

    y_p = _rmsnorm(hp, norm_final, bm=bm_p).reshape(1, s, d)
    y_s = _rmsnorm(hs, norm_final, bm=bm_s).reshape(bs_, 1, d)
    return (y_p, y_s,
            jnp.stack(kd_p), jnp.stack(vd_p), jnp.stack(km_p), jnp.stack(vm_p), jnp.stack(cv_p),
            jnp.stack(kd_s), jnp.stack(vd_s), jnp.stack(km_s), jnp.stack(vm_s), jnp.stack(cv_s))
```

```python
import functools
import math

import jax
import jax.numpy as jnp
from jax import lax
from jax.experimental import pallas as pl
from jax.experimental.pallas import tpu as pltpu

F32 = jnp.float32
BF16 = jnp.bfloat16

N_HEADS = 16
HEAD_DIM = 128
DIFF_DK = HEAD_DIM // 2
MOBA_BLOCK = 256
MOBA_TOPK = 3
PAGE_SIZE = 128
CONV_W = 3
N_BUCKETS = 32
MAX_EXACT = N_BUCKETS // 2
MAX_DISTANCE = 128
EPS = 1e-6
SUBLN_EPS = 1e-5
NEG = -1e30
LOG2E = 1.4426950408889634
ATT_TILE = 512
PAGES_PER_BLOCK = MOBA_BLOCK // PAGE_SIZE
BLOCKS_PER_STEP = 2
PAGES_PER_STEP = PAGES_PER_BLOCK * BLOCKS_PER_STEP
VMEM_LIMIT = 56 * 1024 * 1024

_NT = (((1,), (1,)), ((), ()))


def _lambda_init(i):
    return 0.8 - 0.6 * math.exp(-0.3 * i)


def _params(n_grid_dims):
    return pltpu.CompilerParams(dimension_semantics=("arbitrary",) * n_grid_dims,
                                vmem_limit_bytes=VMEM_LIMIT)


def _norm_rows(x, g):
    r = lax.rsqrt(jnp.mean(x * x, axis=-1, keepdims=True) + EPS)
    return x * r * g


def _norm_matmul_kernel(x_ref, g_ref, w_ref, o_ref, xn_ref):
    @pl.when(pl.program_id(1) == 0)
    def _():
        xn_ref[...] = _norm_rows(x_ref[...], g_ref[...]).astype(BF16)

    o_ref[0] = jnp.dot(xn_ref[...], w_ref[...], preferred_element_type=F32)


def _norm_proj(x, g, w, which, *, bm, bn):
    m, d = x.shape
    nb = d // bn
    return pl.pallas_call(
        _norm_matmul_kernel,
        grid=(m // bm, nb),
        in_specs=[pl.BlockSpec((bm, d), lambda i, j: (i, 0)),
                  pl.BlockSpec((1, d), lambda i, j: (0, 0)),
                  pl.BlockSpec((d, bn), lambda i, j: (0, which * nb + j))],
        out_specs=pl.BlockSpec((1, bm, bn), lambda i, j: (0, i, j)),
        out_shape=jax.ShapeDtypeStruct((1, m, d), F32),
        scratch_shapes=[pltpu.VMEM((bm, d), BF16)],
        compiler_params=_params(2),
        name="norm_proj",
    )(x, g.reshape(1, d), w)


def _mm_res_kernel(a_ref, w_ref, r_ref, o_ref):
    o_ref[...] = r_ref[...] + jnp.dot(a_ref[...], w_ref[...], preferred_element_type=F32)


def _matmul_residual(a, w, res, *, bm, bn):
    m, k = a.shape
    n = w.shape[1]
    return pl.pallas_call(
        _mm_res_kernel,
        grid=(m // bm, n // bn),
        in_specs=[pl.BlockSpec((bm, k), lambda i, j: (i, 0)),
                  pl.BlockSpec((k, bn), lambda i, j: (0, j)),
                  pl.BlockSpec((bm, bn), lambda i, j: (i, j))],
        out_specs=pl.BlockSpec((bm, bn), lambda i, j: (i, j)),
        out_shape=jax.ShapeDtypeStruct((m, n), F32),
        compiler_params=_params(2),
        name="matmul_residual",
    )(a, w, res)


def _rmsnorm_kernel(x_ref, g_ref, o_ref):
    o_ref[...] = _norm_rows(x_ref[...], g_ref[...])


def _rmsnorm(x, g, *, bm):
    m, d = x.shape
    return pl.pallas_call(
        _rmsnorm_kernel,
        grid=(m // bm,),
        in_specs=[pl.BlockSpec((bm, d), lambda i: (i, 0)),
                  pl.BlockSpec((1, d), lambda i: (0, 0))],
        out_specs=pl.BlockSpec((bm, d), lambda i: (i, 0)),
        out_shape=jax.ShapeDtypeStruct((m, d), F32),
        compiler_params=_params(1),
        name="final_rmsnorm",
    )(x, g.reshape(1, d))


FFN_CHUNK = 256


def _silu_gate(gc, u):
    return gc * (1.0 / (1.0 + jnp.exp(-gc))) * u


def _ffn_up_seq_kernel(x_ref, g_ref, wg_ref, wu_ref, cw_ref, cb_ref, h_ref, st_ref,
                       xn_ref, gbuf_ref, carry_ref, *, bm):
    i = pl.program_id(0)
    j = pl.program_id(1)

    @pl.when(j == 0)
    def _():
        xn_ref[...] = _norm_rows(x_ref[...], g_ref[...]).astype(BF16)

    bn = h_ref.shape[1]

    @pl.when(i == 0)
    def _():
        gbuf_ref[0:8, :] = jnp.zeros((8, bn), F32)

    @pl.when(i > 0)
    def _():
        gbuf_ref[0:8, :] = carry_ref[j]

    xn = xn_ref[...]
    cw = cw_ref[...]
    cb = cb_ref[...]
    for c0 in range(0, bn, FFN_CHUNK):
        cs = slice(c0, c0 + FFN_CHUNK)
        g = jnp.dot(xn, wg_ref[:, cs], preferred_element_type=F32)
        u = jnp.dot(xn, wu_ref[:, cs], preferred_element_type=F32)
        gbuf_ref[8:8 + bm, cs] = g
        g1 = gbuf_ref[7:7 + bm, cs]
        g2 = gbuf_ref[6:6 + bm, cs]
        gc = cb[:, cs] + cw[2:3, cs] * g
        gc = gc + cw[0:1, cs] * g2
        gc = gc + cw[1:2, cs] * g1
        h_ref[:, cs] = _silu_gate(gc, u).astype(BF16)
    last8 = gbuf_ref[bm:bm + 8, :]
    carry_ref[j] = last8
    st_ref[0] = last8


def _ffn_up_seq(x, g, w_up, conv_w, conv_b, *, bm, bn):
    m, d = x.shape
    dff = conv_w.shape[1]
    nj = dff // bn
    h, st = pl.pallas_call(
        functools.partial(_ffn_up_seq_kernel, bm=bm),
        grid=(m // bm, nj),
        in_specs=[pl.BlockSpec((bm, d), lambda i, j: (i, 0)),
                  pl.BlockSpec((1, d), lambda i, j: (0, 0)),
                  pl.BlockSpec((d, bn), lambda i, j: (0, j)),
                  pl.BlockSpec((d, bn), lambda i, j: (0, j + nj)),
                  pl.BlockSpec((CONV_W, bn), lambda i, j: (0, j)),
                  pl.BlockSpec((1, bn), lambda i, j: (0, j))],
        out_specs=[pl.BlockSpec((bm, bn), lambda i, j: (i, j)),
                   pl.BlockSpec((1, 8, bn), lambda i, j: (i, 0, j))],
        out_shape=[jax.ShapeDtypeStruct((m, dff), BF16),
                   jax.ShapeDtypeStruct((m // bm, 8, dff), F32)],
        scratch_shapes=[pltpu.VMEM((bm, d), BF16),
                        pltpu.VMEM((bm + 8, bn), F32),
                        pltpu.VMEM((nj, 8, bn), F32)],
        compiler_params=_params(2),
        name="ffn_up_seq",
    )(x, g.reshape(1, d), w_up, w_up, conv_w, conv_b.reshape(1, dff))
    return h, st[-1, 8 - (CONV_W - 1):]


def _ffn_up_tok_kernel(x_ref, g_ref, wg_ref, wu_ref, cw_ref, cb_ref, p0_ref, p1_ref,
                       h_ref, gout_ref, xn_ref):
    @pl.when(pl.program_id(0) == 0)
    def _():
        xn_ref[...] = _norm_rows(x_ref[...], g_ref[...]).astype(BF16)

    xn = xn_ref[...]
    g = jnp.dot(xn, wg_ref[...], preferred_element_type=F32)
    u = jnp.dot(xn, wu_ref[...], preferred_element_type=F32)
    cw = cw_ref[...]
    gc = cb_ref[...] + cw[2:3] * g
    gc = gc + cw[0:1] * p0_ref[...]
    gc = gc + cw[1:2] * p1_ref[...]
    h_ref[...] = _silu_gate(gc, u).astype(BF16)
    gout_ref[...] = g


def _ffn_up_tok(x, g, w_up, conv_w, conv_b, prev, *, bn):
    m, d = x.shape
    dff = conv_w.shape[1]
    nj = dff // bn
    h, gout = pl.pallas_call(
        _ffn_up_tok_kernel,
        grid=(nj,),
        in_specs=[pl.BlockSpec((m, d), lambda j: (0, 0)),
                  pl.BlockSpec((1, d), lambda j: (0, 0)),
                  pl.BlockSpec((d, bn), lambda j: (0, j)),
                  pl.BlockSpec((d, bn), lambda j: (0, j + nj)),
                  pl.BlockSpec((CONV_W, bn), lambda j: (0, j)),
                  pl.BlockSpec((1, bn), lambda j: (0, j)),
                  pl.BlockSpec((m, bn), lambda j: (0, j)),
                  pl.BlockSpec((m, bn), lambda j: (0, j))],
        out_specs=[pl.BlockSpec((m, bn), lambda j: (0, j)),
                   pl.BlockSpec((m, bn), lambda j: (0, j))],
        out_shape=[jax.ShapeDtypeStruct((m, dff), BF16),
                   jax.ShapeDtypeStruct((m, dff), F32)],
        scratch_shapes=[pltpu.VMEM((m, d), BF16)],
        compiler_params=_params(1),
        name="ffn_up_tok",
    )(x, g.reshape(1, d), w_up, w_up, conv_w, conv_b.reshape(1, dff), prev[:, 0], prev[:, 1])
    return h, jnp.stack([prev[:, 1], gout], axis=1)


def _rel_bucket(dist):
    n = jnp.maximum(dist, 0)
    nf = jnp.maximum(n, MAX_EXACT).astype(F32)
    large = MAX_EXACT + (jnp.log(nf / MAX_EXACT) / math.log(MAX_DISTANCE / MAX_EXACT)
                         * (N_BUCKETS - MAX_EXACT)).astype(jnp.int32)
    large = jnp.minimum(large, N_BUCKETS - 1)
    return jnp.where(n < MAX_EXACT, n, large)


def _prompt_bias_by_distance(rel_bias, t):
    assert t >= MAX_DISTANCE
    far = rel_bias[N_BUCKETS - 1]
    by_dist = rel_bias[_rel_bucket(jnp.arange(2 * t))] - far[None, :]
    return by_dist.T.reshape(N_HEADS, 1, 2 * t).astype(F32)


def _decode_bias_tables(rel_bias, past_len):
    far = rel_bias[N_BUCKETS - 1]
    dist = past_len - jnp.arange(past_len)
    pages = (rel_bias[_rel_bucket(dist)] - far[None, :]).reshape(past_len // PAGE_SIZE, PAGE_SIZE, N_HEADS)
    self_bias = (rel_bias[0] - far).reshape(N_HEADS, 1)
    return jnp.transpose(pages, (0, 2, 1)).astype(F32), self_bias.astype(F32)


def _fill_bias_tiles(bb_ref, bd_ref, bs_ref, t):
    w = bb_ref[0] * LOG2E
    lane = lax.broadcasted_iota(jnp.int32, (1, 2 * t), 1)
    w_causal = jnp.where(lane >= t, pltpu.roll(w, t, 1), NEG)
    rolled = pltpu.roll(jnp.broadcast_to(w_causal, (t, 2 * t)), 0, 1, stride=1, stride_axis=0)
    bd_ref[...] = rolled[:, t:]
    rolled = pltpu.roll(jnp.broadcast_to(w, (t, 2 * t)), 0, 1, stride=1, stride_axis=0)
    bs_ref[...] = rolled[:, t:]


def _diff_lambda(lp, lam_init):
    a = jnp.sum(lp[0:1] * lp[1:2], axis=-1, keepdims=True)
    b = jnp.sum(lp[2:3] * lp[3:4], axis=-1, keepdims=True)
    return jnp.exp(a) - jnp.exp(b) + lam_init


ONES_ROWS = 16


def _values_t_with_ones(v):
    return jnp.concatenate([v.T, jnp.ones((ONES_ROWS, v.shape[0]), F32)], axis=0).astype(BF16)


def _flash_first(s, vt, m_ref, acc_ref):
    m = jnp.max(s, axis=0, keepdims=True)
    p = jnp.exp2(s - m)
    m_ref[...] = m
    acc_ref[...] = jnp.dot(vt, p.astype(BF16), preferred_element_type=F32)


def _flash_next(s, vt, m_ref, acc_ref):
    m_prev = m_ref[...]
    m = jnp.maximum(m_prev, jnp.max(s, axis=0, keepdims=True))
    p = jnp.exp2(s - m)
    alpha = jnp.exp2(m_prev - m)
    m_ref[...] = m
    acc_ref[...] = alpha * acc_ref[...] + jnp.dot(vt, p.astype(BF16), preferred_element_type=F32)


def _normalized(acc_ref):
    return acc_ref[0:HEAD_DIM, :] * (1.0 / acc_ref[HEAD_DIM:HEAD_DIM + 1, :])


def _causal_key_tiles(i, scores, vt_ref, bias_diag, bias_sub, chain_a, chain_b):
    @pl.when(i == 0)
    def _():
        _flash_first(scores(i) + bias_diag(), vt_ref[i], *chain_a)

    @pl.when(i > 0)
    def _():
        s_a = scores(i) + bias_diag()
        s_b = scores(i - 1) + bias_sub()
        _flash_first(s_a, vt_ref[i], *chain_a)
        _flash_first(s_b, vt_ref[i - 1], *chain_b)

    def softmax_step(s, m_ref):
        m_prev = m_ref[...]
        m = jnp.maximum(m_prev, jnp.max(s, axis=0, keepdims=True))
        p = jnp.exp2(s - m)
        alpha = jnp.exp2(m_prev - m)
        m_ref[...] = m
        return p.astype(BF16), alpha

    def pair(jj, carry):
        s_a = scores(2 * jj)
        s_b = scores(2 * jj + 1)
        p_a, alpha_a = softmax_step(s_a, chain_a[0])
        pv_a = jnp.dot(vt_ref[2 * jj], p_a, preferred_element_type=F32)
        p_b, alpha_b = softmax_step(s_b, chain_b[0])
        pv_b = jnp.dot(vt_ref[2 * jj + 1], p_b, preferred_element_type=F32)
        chain_a[1][...] = alpha_a * chain_a[1][...] + pv_a
        chain_b[1][...] = alpha_b * chain_b[1][...] + pv_b
        return carry
    lax.fori_loop(0, (i - 1) // 2, pair, 0)

    @pl.when(jnp.logical_and(i >= 2, i % 2 == 0))
    def _():
        _flash_next(scores(i - 2), vt_ref[i - 2], *chain_a)

    @pl.when(i > 0)
    def _():
        (ma_ref, acca_ref), (mb_ref, accb_ref) = chain_a, chain_b
        ma, mb = ma_ref[...], mb_ref[...]
        m = jnp.maximum(ma, mb)
        wa, wb = jnp.exp2(ma - m), jnp.exp2(mb - m)
        ma_ref[...] = m
        acca_ref[...] = wa * acca_ref[...] + wb * accb_ref[...]


def _softmax_chain_scratch(width):
    return [pltpu.VMEM((1, width), F32), pltpu.VMEM((HEAD_DIM + ONES_ROWS, width), F32)]


def _diff_prompt_kernel(q_ref, k_ref, v_ref, bb_ref, lp_ref, sw_ref, o_ref,
                        kb_ref, vt_ref, bd_ref, bs_ref, m_ref, acc_ref, m2_ref, acc2_ref, *, t, lam_init):
    i = pl.program_id(1)
    n_tiles = kb_ref.shape[0]

    @pl.when(i == 0)
    def _():
        _fill_bias_tiles(bb_ref, bd_ref, bs_ref, t)

        def fill(c, carry):
            r0 = pl.multiple_of(c * t, t)
            kb_ref[c] = k_ref[0, pl.ds(r0, t), :].astype(BF16)
            vt_ref[c] = _values_t_with_ones(v_ref[0, pl.ds(r0, t), :])
            return carry
        lax.fori_loop(0, n_tiles, fill, 0)

    q = q_ref[0] * (DIFF_DK ** -0.5 * LOG2E)
    lane = lax.broadcasted_iota(jnp.int32, q.shape, 1)
    q0 = jnp.where(lane < DIFF_DK, q, 0.0)
    q1 = jnp.where(lane >= DIFF_DK, q, 0.0)
    qt = jnp.concatenate([q0.T, q1.T], axis=1).astype(BF16)

    def scores(j):
        return jnp.dot(kb_ref[j], qt, preferred_element_type=F32)

    def both_halves(b_ref):
        return lambda: jnp.concatenate([b_ref[...], b_ref[...]], axis=1)

    _causal_key_tiles(i, scores, vt_ref, both_halves(bd_ref), both_halves(bs_ref),
                      (m_ref, acc_ref), (m2_ref, acc2_ref))

    o2 = _normalized(acc_ref)
    lam = _diff_lambda(lp_ref[...], lam_init)
    o = o2[:, :t] - lam * o2[:, t:]
    o = o * lax.rsqrt(jnp.mean(o * o, axis=0, keepdims=True) + SUBLN_EPS)
    o_ref[...] = (o.T * sw_ref[...] * (1.0 - lam_init)).astype(BF16)


def _diff_prompt_attention(q, k, v, bias_by_dist, lam_params, subln_w, lam_init):
    _, s, d = q.shape
    t = ATT_TILE
    nt = s // t
    return pl.pallas_call(
        functools.partial(_diff_prompt_kernel, t=t, lam_init=lam_init),
        grid=(N_HEADS, nt),
        in_specs=[pl.BlockSpec((1, t, HEAD_DIM), lambda h, i: (0, i, h)),
                  pl.BlockSpec((1, s, HEAD_DIM), lambda h, i: (0, 0, h)),
                  pl.BlockSpec((1, s, HEAD_DIM), lambda h, i: (0, 0, h)),
                  pl.BlockSpec((1, 1, 2 * t), lambda h, i: (h, 0, 0)),
                  pl.BlockSpec((4, DIFF_DK), lambda h, i: (0, 0)),
                  pl.BlockSpec((1, HEAD_DIM), lambda h, i: (0, 0))],
        out_specs=pl.BlockSpec((t, HEAD_DIM), lambda h, i: (i, h)),
        out_shape=jax.ShapeDtypeStruct((s, d), BF16),
        scratch_shapes=[pltpu.VMEM((nt, t, HEAD_DIM), BF16),
                        pltpu.VMEM((nt, HEAD_DIM + ONES_ROWS, t), BF16),
                        pltpu.VMEM((t, t), F32),
                        pltpu.VMEM((t, t), F32)]
                       + 2 * _softmax_chain_scratch(2 * t),
        compiler_params=_params(2),
        name="diff_prompt_attention",
    )(q, k, v, bias_by_dist, lam_params, subln_w.reshape(1, HEAD_DIM))


def _split_bf16(x):
    hi = x.astype(BF16)
    lo = (x - hi.astype(F32)).astype(BF16)
    return hi, lo


def _moba_prompt_kernel(q_ref, k_ref, v_ref, bb_ref, o_ref,
                        ka_ref, vt_ref, km_ref, bd_ref, bs_ref, m_ref, acc_ref, m2_ref, acc2_ref, *, t):
    i = pl.program_id(1)
    n_tiles = ka_ref.shape[0]
    nb = km_ref.shape[0]
    blocks_per_tile = t // MOBA_BLOCK

    @pl.when(i == 0)
    def _():
        _fill_bias_tiles(bb_ref, bd_ref, bs_ref, t)
        for n in range(nb):
            blk = k_ref[0, n * MOBA_BLOCK:(n + 1) * MOBA_BLOCK, :]
            km_ref[n:n + 1, :] = jnp.sum(blk, axis=0, keepdims=True) * (1.0 / MOBA_BLOCK)
        lane = lax.broadcasted_iota(jnp.int32, (t, HEAD_DIM), 1)
        sub = lax.broadcasted_iota(jnp.int32, (t, HEAD_DIM), 0)

        def fill(c, carry):
            r0 = pl.multiple_of(c * t, t)
            ka_ref[c, :, 0:HEAD_DIM] = k_ref[0, pl.ds(r0, t), :].astype(BF16)
            ka_ref[c, :, HEAD_DIM:] = jnp.where(
                lane == c * blocks_per_tile + sub // MOBA_BLOCK, 1.0, 0.0).astype(BF16)
            vt_ref[c] = _values_t_with_ones(v_ref[0, pl.ds(r0, t), :])
            return carry
        lax.fori_loop(0, n_tiles, fill, 0)

    qt = q_ref[0].T

    km_hi, km_lo = _split_bf16(km_ref[...])
    qt_hi, qt_lo = _split_bf16(qt)
    gate = (jnp.dot(km_hi, qt_hi, preferred_element_type=F32)
            + jnp.dot(km_hi, qt_lo, preferred_element_type=F32)
            + jnp.dot(km_lo, qt_hi, preferred_element_type=F32))
    row = lax.broadcasted_iota(jnp.int32, (nb, t), 0)
    own = i * blocks_per_tile + lax.broadcasted_iota(jnp.int32, (nb, t), 1) // MOBA_BLOCK
    rowf = row.astype(F32)
    g = jnp.where(row < own, gate, NEG)
    sel = row == own
    for _ in range(MOBA_TOPK):
        mx = jnp.max(g, axis=0, keepdims=True)
        is_max = (g == mx) & (mx > 0.5 * NEG)
        first = jnp.min(jnp.where(is_max, rowf, float(nb)), axis=0, keepdims=True)
        pick = rowf == first
        sel = sel | pick
        g = jnp.where(pick, NEG, g)
    mask_rows = jnp.where(sel, 0.0, NEG)
    qa = jnp.concatenate([qt * (HEAD_DIM ** -0.5 * LOG2E), mask_rows,
                          jnp.zeros((HEAD_DIM - nb, t), F32)], axis=0).astype(BF16)

    def scores(j):
        return jnp.dot(ka_ref[j], qa, preferred_element_type=F32)

    _causal_key_tiles(i, scores, vt_ref, lambda: bd_ref[...], lambda: bs_ref[...],
                      (m_ref, acc_ref), (m2_ref, acc2_ref))

    o_ref[...] = _normalized(acc_ref).T.astype(BF16)


def _moba_prompt_attention(q, k, v, bias_by_dist):
    _, s, d = q.shape
    t = ATT_TILE
    nt = s // t
    nb = s // MOBA_BLOCK
    assert nb <= HEAD_DIM and t % MOBA_BLOCK == 0
    return pl.pallas_call(
        functools.partial(_moba_prompt_kernel, t=t),
        grid=(N_HEADS, nt),
        in_specs=[pl.BlockSpec((1, t, HEAD_DIM), lambda h, i: (0, i, h)),
                  pl.BlockSpec((1, s, HEAD_DIM), lambda h, i: (0, 0, h)),
                  pl.BlockSpec((1, s, HEAD_DIM), lambda h, i: (0, 0, h)),
                  pl.BlockSpec((1, 1, 2 * t), lambda h, i: (h, 0, 0))],
        out_specs=pl.BlockSpec((t, HEAD_DIM), lambda h, i: (i, h)),
        out_shape=jax.ShapeDtypeStruct((s, d), BF16),
        scratch_shapes=[pltpu.VMEM((nt, t, 2 * HEAD_DIM), BF16),
                        pltpu.VMEM((nt, HEAD_DIM + ONES_ROWS, t), BF16),
                        pltpu.VMEM((nb, HEAD_DIM), F32),
                        pltpu.VMEM((t, t), F32),
                        pltpu.VMEM((t, t), F32)]
                       + 2 * _softmax_chain_scratch(t),
        compiler_params=_params(2),
        name="moba_prompt_attention",
    )(q, k, v, bias_by_dist)


def _page_specs(n_pages, d_rows):
    def spec(g):
        return pl.BlockSpec((1, d_rows, HEAD_DIM),
                            lambda bi, st, pt: (pt[bi * n_pages + st * PAGES_PER_STEP + g], 0, 0))
    return [spec(g) for g in range(PAGES_PER_STEP)]


def _own_head_columns(rows, rows_per_head):
    shape = (rows, PAGE_SIZE * N_HEADS)
    col_head = lax.broadcasted_iota(jnp.int32, shape, 1) % N_HEADS
    return col_head == lax.broadcasted_iota(jnp.int32, shape, 0) // rows_per_head


def _diff_decode_kernel(pt_ref, qb_ref, q2_ref, kn_ref, vn_ref, *rest, lam_init):
    kt_refs = rest[:PAGES_PER_STEP]
    v_refs = rest[PAGES_PER_STEP:2 * PAGES_PER_STEP]
    bp_ref, b0_ref, ex_ref, lp_ref, sw_ref, o_ref, m_ref, l_ref, acc_ref = rest[2 * PAGES_PER_STEP:]
    st = pl.program_id(1)
    rows = 2 * N_HEADS
    scale = DIFF_DK ** -0.5

    @pl.when(st == 0)
    def _():
        m_ref[...] = jnp.sum(q2_ref[0] * scale * kn_ref[0], axis=1, keepdims=True) + b0_ref[...]
        l_ref[...] = jnp.ones((rows, 1), F32)
        acc_ref[...] = vn_ref[0]

    qb = qb_ref[0]
    s_pages = []
    for g in range(PAGES_PER_STEP):
        prod = kt_refs[g][0] * qb
        s_pages.append(jnp.sum(prod.reshape(rows, DIFF_DK, PAGE_SIZE), axis=1) * scale + bp_ref[g])
    s = jnp.concatenate(s_pages, axis=1)
    m_prev = m_ref[...]
    m = jnp.maximum(m_prev, jnp.max(s, axis=1, keepdims=True))
    pr = jnp.exp(s - m)
    alpha = jnp.exp(m_prev - m)
    m_ref[...] = m
    l_ref[...] = alpha * l_ref[...] + jnp.sum(pr, axis=1, keepdims=True)
    prb = pr.astype(BF16)
    own = _own_head_columns(rows, 2)
    pv = None
    for g in range(PAGES_PER_STEP):
        spread = jnp.dot(prb[:, g * PAGE_SIZE:(g + 1) * PAGE_SIZE], ex_ref[...], preferred_element_type=F32)
        part = jnp.dot(jnp.where(own, spread, 0.0).astype(BF16), v_refs[g][0].astype(BF16),
                       preferred_element_type=F32)
        pv = part if pv is None else pv + part
    acc_ref[...] = alpha * acc_ref[...] + pv

    @pl.when(st == pl.num_programs(1) - 1)
    def _():
        acc_ref[...] = acc_ref[...] * (1.0 / l_ref[...])
        lam = _diff_lambda(lp_ref[...], lam_init)
        od = acc_ref[pl.ds(0, N_HEADS, stride=2), :] - lam * acc_ref[pl.ds(1, N_HEADS, stride=2), :]
        od = od * lax.rsqrt(jnp.mean(od * od, axis=1, keepdims=True) + SUBLN_EPS)
        o_ref[0] = (od * sw_ref[...] * (1.0 - lam_init)).astype(BF16)


def _diff_decode_attention(qkv, cache_kt, cache_v, page_table, bias_pages, bias_self,
                           lam_params, subln_w, lam_init):
    _, b, d = qkv.shape
    n_pages = page_table.shape[1]
    assert n_pages % PAGES_PER_STEP == 0
    rows = 2 * N_HEADS
    q, kn, vn = qkv[0], qkv[1], qkv[2]
    qb = jnp.broadcast_to(q[:, :, None], (b, d, PAGE_SIZE))
    q2 = q.reshape(b, rows, DIFF_DK)
    kn2 = kn.reshape(b, rows, DIFF_DK)
    vn2 = jnp.repeat(vn.reshape(b, N_HEADS, HEAD_DIM), 2, axis=1)
    bp = jnp.repeat(bias_pages, 2, axis=1)
    b0 = jnp.repeat(bias_self, 2, axis=0)
    spread = (jnp.arange(PAGE_SIZE)[:, None] == jnp.arange(PAGE_SIZE * N_HEADS)[None, :] // N_HEADS).astype(BF16)

    per_seq = lambda bi, st, pt: (bi, 0, 0)
    const2 = lambda bi, st, pt: (0, 0)
    out = pl.pallas_call(
        functools.partial(_diff_decode_kernel, lam_init=lam_init),
        grid_spec=pltpu.PrefetchScalarGridSpec(
            num_scalar_prefetch=1,
            grid=(b, n_pages // PAGES_PER_STEP),
            in_specs=[pl.BlockSpec((1, d, PAGE_SIZE), per_seq),
                      pl.BlockSpec((1, rows, DIFF_DK), per_seq),
                      pl.BlockSpec((1, rows, DIFF_DK), per_seq),
                      pl.BlockSpec((1, rows, HEAD_DIM), per_seq)]
                     + _page_specs(n_pages, d) + _page_specs(n_pages, d)
                     + [pl.BlockSpec((PAGES_PER_STEP, rows, PAGE_SIZE), lambda bi, st, pt: (st, 0, 0)),
                        pl.BlockSpec((rows, 1), const2),
                        pl.BlockSpec((PAGE_SIZE, PAGE_SIZE * N_HEADS), const2),
                        pl.BlockSpec((4, DIFF_DK), const2),
                        pl.BlockSpec((1, HEAD_DIM), const2)],
            out_specs=pl.BlockSpec((1, N_HEADS, HEAD_DIM), per_seq),
            scratch_shapes=[pltpu.VMEM((rows, 1), F32),
                            pltpu.VMEM((rows, 1), F32),
                            pltpu.VMEM((rows, HEAD_DIM), F32)]),
        out_shape=jax.ShapeDtypeStruct((b, N_HEADS, HEAD_DIM), BF16),
        compiler_params=_params(2),
        name="diff_decode_attention",
    )(page_table.reshape(-1), qb, q2, kn2, vn2,
      *([cache_kt] * PAGES_PER_STEP), *([cache_v] * PAGES_PER_STEP),
      bp, b0, spread, lam_params, subln_w.reshape(1, HEAD_DIM))
    return out.reshape(b, d)


def _moba_decode_kernel(pt_ref, q_ref, kn_ref, vn_ref, *rest):
    k_refs = rest[:PAGES_PER_STEP]
    v_refs = rest[PAGES_PER_STEP:2 * PAGES_PER_STEP]
    bp_ref, b0_ref, o_ref, m_ref, l_ref, acc_ref, ks_ref = rest[2 * PAGES_PER_STEP:]
    st = pl.program_id(1)
    n_blocks = ks_ref.shape[0]
    scale = HEAD_DIM ** -0.5
    cols = PAGE_SIZE * N_HEADS
    own = _own_head_columns(N_HEADS, 1)

    q = q_ref[0]
    qs = (q * scale).astype(BF16)
    for bb in range(BLOCKS_PER_STEP):
        pages = range(bb * PAGES_PER_BLOCK, (bb + 1) * PAGES_PER_BLOCK)
        s_pages = []
        ksum = jnp.zeros((N_HEADS, HEAD_DIM), F32)
        for g in pages:
            k = k_refs[g][0]
            ksum = ksum + jnp.sum(k.reshape(PAGE_SIZE, N_HEADS, HEAD_DIM), axis=0)
            sg = lax.dot_general(qs, k.astype(BF16), _NT, preferred_element_type=F32) + bp_ref[g]
            s_pages.append(jnp.where(own, sg, NEG))
        s = jnp.concatenate(s_pages, axis=1)
        m = jnp.max(s, axis=1, keepdims=True)
        pr = jnp.exp(s - m)
        prb = pr.astype(BF16)
        pv = None
        for n, g in enumerate(pages):
            part = jnp.dot(prb[:, n * cols:(n + 1) * cols], v_refs[g][0].astype(BF16),
                           preferred_element_type=F32)
            pv = part if pv is None else pv + part
        blk = st * BLOCKS_PER_STEP + bb
        m_ref[blk] = m
        l_ref[blk] = jnp.sum(pr, axis=1, keepdims=True)
        acc_ref[blk] = pv
        ks_ref[blk] = ksum

    @pl.when(st == pl.num_programs(1) - 1)
    def _():
        gates = [jnp.sum(q * (ks_ref[n] * (1.0 / MOBA_BLOCK)), axis=1, keepdims=True)
                 for n in range(n_blocks)]
        s_self = jnp.sum(q * scale * kn_ref[0], axis=1, keepdims=True) + b0_ref[...]
        m_tot = s_self
        sels = []
        for n in range(n_blocks):
            rank = jnp.zeros((N_HEADS, 1), F32)
            for n2 in range(n_blocks):
                if n2 < n:
                    rank = rank + jnp.where(gates[n2] >= gates[n], 1.0, 0.0)
                elif n2 > n:
                    rank = rank + jnp.where(gates[n2] > gates[n], 1.0, 0.0)
            sel = rank < float(min(MOBA_TOPK, n_blocks))
            sels.append(sel)
            m_tot = jnp.maximum(m_tot, jnp.where(sel, m_ref[n], NEG))
        w_self = jnp.exp(s_self - m_tot)
        l_tot = w_self
        o = w_self * vn_ref[0]
        for n in range(n_blocks):
            w = jnp.where(sels[n], jnp.exp(m_ref[n] - m_tot), 0.0)
            l_tot = l_tot + w * l_ref[n]
            o = o + w * acc_ref[n]
        o_ref[0] = (o * (1.0 / l_tot)).astype(BF16)


def _moba_decode_attention(qkv, cache_k, cache_v, page_table, bias_pages, bias_self):
    _, b, d = qkv.shape
    n_pages = page_table.shape[1]
    assert n_pages % PAGES_PER_STEP == 0
    n_blocks = n_pages // PAGES_PER_BLOCK
    q3 = qkv.reshape(3, b, N_HEADS, HEAD_DIM)

    def qkv_spec(which):
        return pl.BlockSpec((None, 1, N_HEADS, HEAD_DIM), lambda bi, st, pt: (which, bi, 0, 0))

    out = pl.pallas_call(
        _moba_decode_kernel,
        grid_spec=pltpu.PrefetchScalarGridSpec(
            num_scalar_prefetch=1,
            grid=(b, n_pages // PAGES_PER_STEP),
            in_specs=[qkv_spec(0), qkv_spec(1), qkv_spec(2)]
                     + _page_specs(n_pages, d) + _page_specs(n_pages, d)
                     + [pl.BlockSpec((PAGES_PER_STEP, N_HEADS, PAGE_SIZE * N_HEADS),
                                     lambda bi, st, pt: (st, 0, 0)),
                        pl.BlockSpec((N_HEADS, 1), lambda bi, st, pt: (0, 0))],
            out_specs=pl.BlockSpec((1, N_HEADS, HEAD_DIM), lambda bi, st, pt: (bi, 0, 0)),
            scratch_shapes=[pltpu.VMEM((n_blocks, N_HEADS, 1), F32),
                            pltpu.VMEM((n_blocks, N_HEADS, 1), F32),
                            pltpu.VMEM((n_blocks, N_HEADS, HEAD_DIM), F32),
                            pltpu.VMEM((n_blocks, N_HEADS, HEAD_DIM), F32)]),
        out_shape=jax.ShapeDtypeStruct((b, N_HEADS, HEAD_DIM), BF16),
        compiler_params=_params(2),
        name="moba_decode_attention",
    )(page_table.reshape(-1), q3, q3, q3,
      *([cache_k] * PAGES_PER_STEP), *([cache_v] * PAGES_PER_STEP),
      jnp.repeat(bias_pages, N_HEADS, axis=2), bias_self)
    return out.reshape(b, d)


def _pages_pos_head(cache):
    n_phys = cache.shape[0]
    return cache.reshape(n_phys, PAGE_SIZE * N_HEADS, HEAD_DIM)


def _pages_key_major(cache):
    n_phys = cache.shape[0]
    return jnp.transpose(cache, (0, 2, 3, 4, 1)).reshape(n_phys, N_HEADS * HEAD_DIM, PAGE_SIZE)


def _row_block(m):
    return 512 if m % 512 == 0 else m


def kernel(x_prompt, x_sample, cache_k_diff, cache_v_diff, cache_k_moba, cache_v_moba, state_conv, page_table, rel_bias, norm_mix, norm_ffn, norm_final, w_qkv_diff, lambda_diff, subln_diff, w_o_diff, w_qkv_moba, w_o_moba, w_up, conv_w, conv_b, w_down):
    bp_, s, d = x_prompt.shape
    bs_, ts, _ = x_sample.shape
    assert bp_ == 1 and ts == 1 and d == N_HEADS * HEAD_DIM
    depth = norm_mix.shape[0]
    past_len = page_table.shape[1] * PAGE_SIZE
    dff = conv_w.shape[-1]
    bn_ff = 512 if dff % 512 == 0 else dff

    hp = x_prompt.reshape(s, d)
    hs = x_sample.reshape(bs_, d)
    bm_p, bm_s = _row_block(s), _row_block(bs_)

    bias_by_dist = _prompt_bias_by_distance(rel_bias, ATT_TILE)
    bias_pages, bias_self = _decode_bias_tables(rel_bias, past_len)

    kd_p, vd_p, km_p, vm_p, cv_p = [], [], [], [], []
    kd_s, vd_s, km_s, vm_s, cv_s = [], [], [], [], []
    for i in range(depth):
        slot = i // 2
        if i % 2 == 0:
            li = _lambda_init(i)
            w_qkv = w_qkv_diff[slot].astype(BF16)
            w_o = w_o_diff[slot].astype(BF16)
            q_p, k_p, v_p = [_norm_proj(hp, norm_mix[i], w_qkv, n, bm=bm_p, bn=1024) for n in range(3)]
            qkv_s = jnp.concatenate([_norm_proj(hs, norm_mix[i], w_qkv, n, bm=bm_s, bn=1024)
                                     for n in range(3)], axis=0)
            o_p = _diff_prompt_attention(q_p, k_p, v_p, bias_by_dist, lambda_diff[slot], subln_diff[slot], li)
            o_s = _diff_decode_attention(qkv_s, _pages_key_major(cache_k_diff[slot]),
                                         _pages_pos_head(cache_v_diff[slot]), page_table,
                                         bias_pages, bias_self, lambda_diff[slot], subln_diff[slot], li)
            kd_p.append(k_p.reshape(1, s, N_HEADS, 2, DIFF_DK))
            vd_p.append(v_p.reshape(1, s, N_HEADS, HEAD_DIM))
            kd_s.append(qkv_s[1].reshape(bs_, 1, N_HEADS, 2, DIFF_DK))
            vd_s.append(qkv_s[2].reshape(bs_, 1, N_HEADS, HEAD_DIM))
        else:
            w_qkv = w_qkv_moba[slot].astype(BF16)
            w_o = w_o_moba[slot].astype(BF16)
            q_p, k_p, v_p = [_norm_proj(hp, norm_mix[i], w_qkv, n, bm=bm_p, bn=1024) for n in range(3)]
            qkv_s = jnp.concatenate([_norm_proj(hs, norm_mix[i], w_qkv, n, bm=bm_s, bn=1024)
                                     for n in range(3)], axis=0)
            o_p = _moba_prompt_attention(q_p, k_p, v_p, bias_by_dist)
            o_s = _moba_decode_attention(qkv_s, _pages_pos_head(cache_k_moba[slot]),
                                         _pages_pos_head(cache_v_moba[slot]), page_table,
                                         bias_pages, bias_self)
            km_p.append(k_p.reshape(1, s, N_HEADS, HEAD_DIM))
            vm_p.append(v_p.reshape(1, s, N_HEADS, HEAD_DIM))
            km_s.append(qkv_s[1].reshape(bs_, 1, N_HEADS, HEAD_DIM))
            vm_s.append(qkv_s[2].reshape(bs_, 1, N_HEADS, HEAD_DIM))
        hp = _matmul_residual(o_p, w_o, hp, bm=bm_p, bn=512)
        hs = _matmul_residual(o_s, w_o, hs, bm=bm_s, bn=512)

        wu = w_up[i].astype(BF16)
        wd = w_down[i].astype(BF16)
        f_p, st_p = _ffn_up_seq(hp, norm_ffn[i], wu, conv_w[i], conv_b[i], bm=bm_p, bn=bn_ff)
        f_s, st_s = _ffn_up_tok(hs, norm_ffn[i], wu, conv_w[i], conv_b[i], state_conv[i], bn=bn_ff)
        hp = _matmul_residual(f_p, wd, hp, bm=bm_p, bn=512)
        hs = _matmul_residual(f_s, wd, hs, bm=bm_s, bn=512)
        cv_p.append(st_p.reshape(1, CONV_W - 1, dff))
        cv_s.append(st_s)

    y_p = _rmsnorm(hp, norm_final, bm=bm_p).reshape(1, s, d)
    y_s = _rmsnorm(hs, norm_final, bm=bm_s).reshape(bs_, 1, d)
    return (y_p, y_s,
            jnp.stack(kd_p), jnp.stack(vd_p), jnp.stack(km_p), jnp.stack(vm_p), jnp.stack(cv_p),
            jnp.stack(kd_s), jnp.stack(vd_s), jnp.stack(km_s), jnp.stack(vm_s), jnp.stack(cv_s))
```
